```python
import math
import jax, jax.numpy as jnp
from jax import lax
import numpy as np

D_MODEL = 2048
BATCH = 8
SEQ = 8192
DEPTH = 4

PLE_DIM = 256
D_A = D_MODEL // 2
A_GROUPS = 8
CONV_A_WIDTH = 31
D_B = D_MODEL // 2
RG_HEADS = 8
RG_HEAD_DIM = D_B // RG_HEADS
CONV_B_WIDTH = 4
RG_C = 8.0
SB_HEADS = 16
SB_HEAD_DIM = D_MODEL // SB_HEADS
BLOCK_Q = 128
D_FF = 4 * D_MODEL
N_EVEN = (DEPTH + 1) // 2
N_ODD = DEPTH // 2
EPS = 1e-6
D_IN_REC = 2 * D_A + 2 * D_B

kernel_name = "hybrid_conformer_rglru_stickbreaking_trunk"


def rms_norm(x, g):
    xf = x.astype(jnp.float32)
    y = xf * lax.rsqrt(jnp.mean(xf * xf, axis=-1, keepdims=True) + EPS)
    return (y * g.astype(jnp.float32)).astype(x.dtype)


def layer_norm(x, g, b):
    xf = x.astype(jnp.float32)
    mu = jnp.mean(xf, axis=-1, keepdims=True)
    var = jnp.mean(jnp.square(xf - mu), axis=-1, keepdims=True)
    y = (xf - mu) * lax.rsqrt(var + EPS)
    return (y * g.astype(jnp.float32) + b.astype(jnp.float32)).astype(x.dtype)


def causal_depthwise_conv(x, w, b):
    k_width = w.shape[0]
    out = lax.conv_general_dilated(
        x, w[:, None, :].astype(x.dtype), window_strides=(1,), padding=[(k_width - 1, 0)],
        dimension_numbers=("NWC", "WIO", "NWC"), feature_group_count=x.shape[-1])
    return out + b


def rg_lru(x, w_a, b_a, w_x, b_x, lam):
    bsz, seq, ch = x.shape
    xh = x.reshape(bsz, seq, RG_HEADS, RG_HEAD_DIM)
    r = jax.nn.sigmoid(jnp.einsum("bshi,hij->bshj", xh, w_a).reshape(bsz, seq, ch) + b_a)
    i = jax.nn.sigmoid(jnp.einsum("bshi,hij->bshj", xh, w_x).reshape(bsz, seq, ch) + b_x)
    log_a = -RG_C * r.astype(jnp.float32) * jax.nn.softplus(-lam.astype(jnp.float32))
    a = jnp.exp(log_a)
    mult = jnp.sqrt(-jnp.expm1(2.0 * log_a))
    u = mult * (i * x).astype(jnp.float32)

    def combine(c1, c2):
        a1, u1 = c1
        a2, u2 = c2
        return a1 * a2, a2 * u1 + u2

    _, h = lax.associative_scan(combine, (a, u), axis=1)
    return h.astype(x.dtype)


def conv_recurrent_mixer(hn, w_in, conv_a_w, conv_a_b, ln_a_g, ln_a_b,
                         conv_b_w, conv_b_b, w_rg_a, b_rg_a, w_rg_x, b_rg_x, rg_lambda, w_out):
    u = hn @ w_in
    a_val, a_gate, xr, gr = jnp.split(u, [D_A, 2 * D_A, 2 * D_A + D_B], axis=-1)
    ya = a_val * jax.nn.sigmoid(a_gate)
    ya = causal_depthwise_conv(ya, conv_a_w, conv_a_b)
    ya = jax.nn.silu(layer_norm(ya, ln_a_g, ln_a_b))
    xr = causal_depthwise_conv(xr, conv_b_w, conv_b_b)
    yb = rg_lru(xr, w_rg_a, b_rg_a, w_rg_x, b_rg_x, rg_lambda) * jax.nn.gelu(gr)
    return jnp.concatenate([ya, yb], axis=-1) @ w_out


def stick_breaking_attention(q, k, v):
    bsz, nh, seq, dh = q.shape
    n_blocks = seq // BLOCK_Q
    scale = 1.0 / math.sqrt(dh)
    q_blocks = q.reshape(bsz, nh, n_blocks, BLOCK_Q, dh).transpose(2, 0, 1, 3, 4)
    key_pos = jnp.arange(seq)

    def one_block(args):
        q_blk, blk_idx = args
        z = jnp.einsum("bhqd,bhkd->bhqk", q_blk, k,
                       preferred_element_type=jnp.float32) * scale
        q_pos = blk_idx * BLOCK_Q + jnp.arange(BLOCK_Q)
        mask = key_pos[None, :] < q_pos[:, None]
        log_one_minus = jnp.where(mask, -jax.nn.softplus(z), 0.0)
        key_axis = log_one_minus.ndim - 1
        tail = lax.cumsum(log_one_minus, axis=key_axis, reverse=True) - log_one_minus
        log_w = jax.nn.log_sigmoid(z) + tail
        w = jnp.where(mask, jnp.exp(log_w), 0.0)
        return jnp.einsum("bhqk,bhkd->bhqd", w.astype(v.dtype), v)

    out = lax.map(one_block, (q_blocks, jnp.arange(n_blocks)))
    return out.transpose(1, 2, 0, 3, 4).reshape(bsz, nh, seq, dh)


def stick_breaking_mixer(hn, w_qkv, w_o):
    bsz, seq, _ = hn.shape
    qkv = hn @ w_qkv
    q, k, v = jnp.split(qkv, 3, axis=-1)
    to_heads = lambda t: t.reshape(bsz, seq, SB_HEADS, SB_HEAD_DIM).transpose(0, 2, 1, 3)
    o = stick_breaking_attention(to_heads(q), to_heads(k), to_heads(v))
    o = o.transpose(0, 2, 1, 3).reshape(bsz, seq, D_MODEL)
    return o @ w_o


def squared_relu_mlp(hn, w_up, w_down):
    return jnp.square(jax.nn.relu(hn @ w_up)) @ w_down


def _fwd_setup_inputs(seed: int = 0) -> dict:
    key = jax.random.key(seed)
    ks = jax.random.split(key, 32)
    f32 = jnp.float32
    nrm = lambda k, shape, fan_in: jax.random.normal(k, shape, f32) * (fan_in ** -0.5)
    gain = lambda k, shape: 1.0 + 0.05 * jax.random.normal(k, shape, f32)
    small = lambda k, shape: 0.02 * jax.random.normal(k, shape, f32)
    a_c = jax.random.uniform(ks[16], (N_EVEN, D_B), f32, 0.9, 0.999)
    a0 = a_c ** (1.0 / RG_C)
    rg_lambda = jnp.log(a0) - jnp.log1p(-a0)
    return {
        "x": jax.random.normal(ks[0], (BATCH, SEQ, D_MODEL), f32),
        "p": jax.random.normal(ks[1], (DEPTH, BATCH, SEQ, PLE_DIM), f32),
        "norm_mix_g": gain(ks[2], (DEPTH, D_MODEL)),
        "norm_mlp_g": gain(ks[3], (DEPTH, D_MODEL)),
        "norm_ple_g": gain(ks[4], (DEPTH, D_MODEL)),
        "norm_f_g": gain(ks[5], (D_MODEL,)),
        "w_in_rec": nrm(ks[6], (N_EVEN, D_MODEL, D_IN_REC), D_MODEL),
        "conv_a_w": nrm(ks[7], (N_EVEN, CONV_A_WIDTH, D_A), CONV_A_WIDTH),
        "conv_a_b": small(ks[8], (N_EVEN, D_A)),
        "ln_a_g": gain(ks[9], (N_EVEN, D_A)),
        "ln_a_b": small(ks[10], (N_EVEN, D_A)),
        "conv_b_w": nrm(ks[11], (N_EVEN, CONV_B_WIDTH, D_B), CONV_B_WIDTH),
        "conv_b_b": small(ks[12], (N_EVEN, D_B)),
        "w_rg_a": nrm(ks[13], (N_EVEN, RG_HEADS, RG_HEAD_DIM, RG_HEAD_DIM), RG_HEAD_DIM),
        "b_rg_a": small(ks[14], (N_EVEN, D_B)),
        "w_rg_x": nrm(ks[15], (N_EVEN, RG_HEADS, RG_HEAD_DIM, RG_HEAD_DIM), RG_HEAD_DIM),
        "b_rg_x": small(ks[17], (N_EVEN, D_B)),
        "rg_lambda": rg_lambda,
        "w_out_rec": nrm(ks[18], (N_EVEN, D_A + D_B, D_MODEL), D_A + D_B),
        "w_qkv": nrm(ks[19], (N_ODD, D_MODEL, 3 * D_MODEL), D_MODEL),
        "w_o_attn": nrm(ks[20], (N_ODD, D_MODEL, D_MODEL), D_MODEL),
        "w_mlp_up": nrm(ks[21], (DEPTH, D_MODEL, D_FF), D_MODEL),
        "w_mlp_down": nrm(ks[22], (DEPTH, D_FF, D_MODEL), D_FF),
        "w_ple_proj": nrm(ks[23], (DEPTH, PLE_DIM, D_MODEL), PLE_DIM),
        "w_ple_gate": nrm(ks[24], (DEPTH, D_MODEL, D_MODEL), D_MODEL),
    }


def _fwd_reference(x, p, norm_mix_g, norm_mlp_g, norm_ple_g, norm_f_g, w_in_rec, conv_a_w, conv_a_b,
              ln_a_g, ln_a_b, conv_b_w, conv_b_b, w_rg_a, b_rg_a, w_rg_x, b_rg_x, rg_lambda,
              w_out_rec, w_qkv, w_o_attn, w_mlp_up, w_mlp_down, w_ple_proj, w_ple_gate):
    h = x
    for i in range(DEPTH):
        j = i // 2
        hn = rms_norm(h, norm_mix_g[i])
        if i % 2 == 0:
            mix = conv_recurrent_mixer(hn, w_in_rec[j], conv_a_w[j], conv_a_b[j], ln_a_g[j], ln_a_b[j],
                                       conv_b_w[j], conv_b_b[j], w_rg_a[j], b_rg_a[j], w_rg_x[j],
                                       b_rg_x[j], rg_lambda[j], w_out_rec[j])
        else:
            mix = stick_breaking_mixer(hn, w_qkv[j], w_o_attn[j])
        h = h + mix
        h = h + squared_relu_mlp(rms_norm(h, norm_mlp_g[i]), w_mlp_up[i], w_mlp_down[i])
        gate = jax.nn.sigmoid(rms_norm(h, norm_ple_g[i]) @ w_ple_gate[i])
        h = h + (p[i] @ w_ple_proj[i]) * gate
    return rms_norm(h, norm_f_g)


import jax as _jax
import jax.numpy as _jnp

TWIN_FORMAT = 'train_step'
FWD_PARAMS = ['x', 'p', 'norm_mix_g', 'norm_mlp_g', 'norm_ple_g', 'norm_f_g', 'w_in_rec', 'conv_a_w', 'conv_a_b', 'ln_a_g', 'ln_a_b', 'conv_b_w', 'conv_b_b', 'w_rg_a', 'b_rg_a', 'w_rg_x', 'b_rg_x', 'rg_lambda', 'w_out_rec', 'w_qkv', 'w_o_attn', 'w_mlp_up', 'w_mlp_down', 'w_ple_proj', 'w_ple_gate']
TWIN_WEIGHTS = ['norm_mix_g', 'norm_mlp_g', 'norm_ple_g', 'norm_f_g', 'w_in_rec', 'conv_a_w', 'conv_a_b', 'ln_a_g', 'ln_a_b', 'conv_b_w', 'conv_b_b', 'w_rg_a', 'b_rg_a', 'w_rg_x', 'b_rg_x', 'rg_lambda', 'w_out_rec', 'w_qkv', 'w_o_attn', 'w_mlp_up', 'w_mlp_down', 'w_ple_proj', 'w_ple_gate']
TWIN_DIFF_INPUT = 'x'
TWIN_INPUTS = ['x', 'p', 'norm_mix_g', 'norm_mlp_g', 'norm_ple_g', 'norm_f_g', 'w_in_rec', 'conv_a_w', 'conv_a_b', 'ln_a_g', 'ln_a_b', 'conv_b_w', 'conv_b_b', 'w_rg_a', 'b_rg_a', 'w_rg_x', 'b_rg_x', 'rg_lambda', 'w_out_rec', 'w_qkv', 'w_o_attn', 'w_mlp_up', 'w_mlp_down', 'w_ple_proj', 'w_ple_gate', 'loss_target', 'm_norm_mix_g', 'm_norm_mlp_g', 'm_norm_ple_g', 'm_norm_f_g', 'm_w_in_rec', 'm_conv_a_w', 'm_conv_a_b', 'm_ln_a_g', 'm_ln_a_b', 'm_conv_b_w', 'm_conv_b_b', 'm_w_rg_a', 'm_b_rg_a', 'm_w_rg_x', 'm_b_rg_x', 'm_rg_lambda', 'm_w_out_rec', 'm_w_qkv', 'm_w_o_attn', 'm_w_mlp_up', 'm_w_mlp_down', 'm_w_ple_proj', 'm_w_ple_gate', 'v_norm_mix_g', 'v_norm_mlp_g', 'v_norm_ple_g', 'v_norm_f_g', 'v_w_in_rec', 'v_conv_a_w', 'v_conv_a_b', 'v_ln_a_g', 'v_ln_a_b', 'v_conv_b_w', 'v_conv_b_b', 'v_w_rg_a', 'v_b_rg_a', 'v_w_rg_x', 'v_b_rg_x', 'v_rg_lambda', 'v_w_out_rec', 'v_w_qkv', 'v_w_o_attn', 'v_w_mlp_up', 'v_w_mlp_down', 'v_w_ple_proj', 'v_w_ple_gate']
TWIN_OUTPUTS = ['loss', 'grad_x', 'grad_norm_mix_g', 'grad_norm_mlp_g', 'grad_norm_ple_g', 'grad_norm_f_g', 'grad_w_in_rec', 'grad_conv_a_w', 'grad_conv_a_b', 'grad_ln_a_g', 'grad_ln_a_b', 'grad_conv_b_w', 'grad_conv_b_b', 'grad_w_rg_a', 'grad_b_rg_a', 'grad_w_rg_x', 'grad_b_rg_x', 'grad_rg_lambda', 'grad_w_out_rec', 'grad_w_qkv', 'grad_w_o_attn', 'grad_w_mlp_up', 'grad_w_mlp_down', 'grad_w_ple_proj', 'grad_w_ple_gate', 'delta_norm_mix_g', 'delta_norm_mlp_g', 'delta_norm_ple_g', 'delta_norm_f_g', 'delta_w_in_rec', 'delta_conv_a_w', 'delta_conv_a_b', 'delta_ln_a_g', 'delta_ln_a_b', 'delta_conv_b_w', 'delta_conv_b_b', 'delta_w_rg_a', 'delta_b_rg_a', 'delta_w_rg_x', 'delta_b_rg_x', 'delta_rg_lambda', 'delta_w_out_rec', 'delta_w_qkv', 'delta_w_o_attn', 'delta_w_mlp_up', 'delta_w_mlp_down', 'delta_w_ple_proj', 'delta_w_ple_gate', 'new_m_norm_mix_g', 'new_m_norm_mlp_g', 'new_m_norm_ple_g', 'new_m_norm_f_g', 'new_m_w_in_rec', 'new_m_conv_a_w', 'new_m_conv_a_b', 'new_m_ln_a_g', 'new_m_ln_a_b', 'new_m_conv_b_w', 'new_m_conv_b_b', 'new_m_w_rg_a', 'new_m_b_rg_a', 'new_m_w_rg_x', 'new_m_b_rg_x', 'new_m_rg_lambda', 'new_m_w_out_rec', 'new_m_w_qkv', 'new_m_w_o_attn', 'new_m_w_mlp_up', 'new_m_w_mlp_down', 'new_m_w_ple_proj', 'new_m_w_ple_gate', 'new_v_norm_mix_g', 'new_v_norm_mlp_g', 'new_v_norm_ple_g', 'new_v_norm_f_g', 'new_v_w_in_rec', 'new_v_conv_a_w', 'new_v_conv_a_b', 'new_v_ln_a_g', 'new_v_ln_a_b', 'new_v_conv_b_w', 'new_v_conv_b_b', 'new_v_w_rg_a', 'new_v_b_rg_a', 'new_v_w_rg_x', 'new_v_b_rg_x', 'new_v_rg_lambda', 'new_v_w_out_rec', 'new_v_w_qkv', 'new_v_w_o_attn', 'new_v_w_mlp_up', 'new_v_w_mlp_down', 'new_v_w_ple_proj', 'new_v_w_ple_gate']
TWIN_LEAF_KINDS = {'loss': 'loss', 'grad_x': 'grad_x', 'grad_norm_mix_g': 'grad_w', 'grad_norm_mlp_g': 'grad_w', 'grad_norm_ple_g': 'grad_w', 'grad_norm_f_g': 'grad_w', 'grad_w_in_rec': 'grad_w', 'grad_conv_a_w': 'grad_w', 'grad_conv_a_b': 'grad_w', 'grad_ln_a_g': 'grad_w', 'grad_ln_a_b': 'grad_w', 'grad_conv_b_w': 'grad_w', 'grad_conv_b_b': 'grad_w', 'grad_w_rg_a': 'grad_w', 'grad_b_rg_a': 'grad_w', 'grad_w_rg_x': 'grad_w', 'grad_b_rg_x': 'grad_w', 'grad_rg_lambda': 'grad_w', 'grad_w_out_rec': 'grad_w', 'grad_w_qkv': 'grad_w', 'grad_w_o_attn': 'grad_w', 'grad_w_mlp_up': 'grad_w', 'grad_w_mlp_down': 'grad_w', 'grad_w_ple_proj': 'grad_w', 'grad_w_ple_gate': 'grad_w', 'delta_norm_mix_g': 'delta_w', 'delta_norm_mlp_g': 'delta_w', 'delta_norm_ple_g': 'delta_w', 'delta_norm_f_g': 'delta_w', 'delta_w_in_rec': 'delta_w', 'delta_conv_a_w': 'delta_w', 'delta_conv_a_b': 'delta_w', 'delta_ln_a_g': 'delta_w', 'delta_ln_a_b': 'delta_w', 'delta_conv_b_w': 'delta_w', 'delta_conv_b_b': 'delta_w', 'delta_w_rg_a': 'delta_w', 'delta_b_rg_a': 'delta_w', 'delta_w_rg_x': 'delta_w', 'delta_b_rg_x': 'delta_w', 'delta_rg_lambda': 'delta_w', 'delta_w_out_rec': 'delta_w', 'delta_w_qkv': 'delta_w', 'delta_w_o_attn': 'delta_w', 'delta_w_mlp_up': 'delta_w', 'delta_w_mlp_down': 'delta_w', 'delta_w_ple_proj': 'delta_w', 'delta_w_ple_gate': 'delta_w', 'new_m_norm_mix_g': 'new_m', 'new_m_norm_mlp_g': 'new_m', 'new_m_norm_ple_g': 'new_m', 'new_m_norm_f_g': 'new_m', 'new_m_w_in_rec': 'new_m', 'new_m_conv_a_w': 'new_m', 'new_m_conv_a_b': 'new_m', 'new_m_ln_a_g': 'new_m', 'new_m_ln_a_b': 'new_m', 'new_m_conv_b_w': 'new_m', 'new_m_conv_b_b': 'new_m', 'new_m_w_rg_a': 'new_m', 'new_m_b_rg_a': 'new_m', 'new_m_w_rg_x': 'new_m', 'new_m_b_rg_x': 'new_m', 'new_m_rg_lambda': 'new_m', 'new_m_w_out_rec': 'new_m', 'new_m_w_qkv': 'new_m', 'new_m_w_o_attn': 'new_m', 'new_m_w_mlp_up': 'new_m', 'new_m_w_mlp_down': 'new_m', 'new_m_w_ple_proj': 'new_m', 'new_m_w_ple_gate': 'new_m', 'new_v_norm_mix_g': 'new_v', 'new_v_norm_mlp_g': 'new_v', 'new_v_norm_ple_g': 'new_v', 'new_v_norm_f_g': 'new_v', 'new_v_w_in_rec': 'new_v', 'new_v_conv_a_w': 'new_v', 'new_v_conv_a_b': 'new_v', 'new_v_ln_a_g': 'new_v', 'new_v_ln_a_b': 'new_v', 'new_v_conv_b_w': 'new_v', 'new_v_conv_b_b': 'new_v', 'new_v_w_rg_a': 'new_v', 'new_v_b_rg_a': 'new_v', 'new_v_w_rg_x': 'new_v', 'new_v_b_rg_x': 'new_v', 'new_v_rg_lambda': 'new_v', 'new_v_w_out_rec': 'new_v', 'new_v_w_qkv': 'new_v', 'new_v_w_o_attn': 'new_v', 'new_v_w_mlp_up': 'new_v', 'new_v_w_mlp_down': 'new_v', 'new_v_w_ple_proj': 'new_v', 'new_v_w_ple_gate': 'new_v'}


def _forward(args):
    return _fwd_reference(*[args[k] for k in FWD_PARAMS])


def _output_shape():
    def fwd():
        inp = _fwd_setup_inputs(0)
        return _fwd_reference(*[inp[k] for k in FWD_PARAMS])
    out = _jax.eval_shape(fwd)
    return out.shape, out.dtype

N_MICROBATCH = 1
ADAM_LR = 0.001
ADAM_B1 = 0.9
ADAM_B2 = 0.999
ADAM_EPS = 1e-08
ADAM_WD = 0.01
ADAM_STEP = 10
PER_EXAMPLE_BATCH_AXIS = {'x': 0, 'p': 1, 'loss_target': 0}
SHARED_INPUTS = []
_WEIGHT_DTYPES = {'norm_mix_g': _jnp.float32, 'norm_mlp_g': _jnp.float32, 'norm_ple_g': _jnp.float32, 'norm_f_g': _jnp.float32, 'w_in_rec': _jnp.float32, 'conv_a_w': _jnp.float32, 'conv_a_b': _jnp.float32, 'ln_a_g': _jnp.float32, 'ln_a_b': _jnp.float32, 'conv_b_w': _jnp.float32, 'conv_b_b': _jnp.float32, 'w_rg_a': _jnp.float32, 'b_rg_a': _jnp.float32, 'w_rg_x': _jnp.float32, 'b_rg_x': _jnp.float32, 'rg_lambda': _jnp.float32, 'w_out_rec': _jnp.float32, 'w_qkv': _jnp.float32, 'w_o_attn': _jnp.float32, 'w_mlp_up': _jnp.float32, 'w_mlp_down': _jnp.float32, 'w_ple_proj': _jnp.float32, 'w_ple_gate': _jnp.float32}
MOMENT_SCALE = {'norm_mix_g': 1.592160e-01, 'norm_mlp_g': 1.191069e-01, 'norm_ple_g': 1.426572e-02, 'norm_f_g': 3.300361e+01, 'w_in_rec': 1.149774e-01, 'conv_a_w': 7.242209e-02, 'conv_a_b': 4.870308e-01, 'ln_a_g': 1.889195e-01, 'ln_a_b': 2.979719e-01, 'conv_b_w': 2.045232e-01, 'conv_b_b': 1.349169e+00, 'w_rg_a': 3.228617e-02, 'b_rg_a': 4.453037e-02, 'w_rg_x': 6.482291e-02, 'b_rg_x': 8.615152e-02, 'rg_lambda': 1.077742e-01, 'w_out_rec': 1.612782e-01, 'w_qkv': 8.085945e-02, 'w_o_attn': 1.479404e-01, 'w_mlp_up': 5.852609e-02, 'w_mlp_down': 2.413319e-01, 'w_ple_proj': 3.089464e-02, 'w_ple_gate': 1.356730e-02}


def _to_microbatches(a, axis):
    t = _jnp.moveaxis(a, axis, 0)
    t = t.reshape((N_MICROBATCH, t.shape[0] // N_MICROBATCH) + t.shape[1:])
    return _jnp.moveaxis(t, 1, axis + 1)


def setup_inputs(seed: int = 0) -> dict:
    inp = _fwd_setup_inputs(seed)
    key = _jax.random.fold_in(_jax.random.key(seed), 7919)
    shape, _ = _output_shape()
    out = dict(inp)
    out["loss_target"] = _jax.random.normal(_jax.random.fold_in(key, 0), shape, _jnp.float32)
    for i, name in enumerate(TWIN_WEIGHTS):
        w = inp[name].astype(_jnp.float32)
        if MOMENT_SCALE is None:
            s = _jnp.sqrt(_jnp.mean(_jnp.square(w)) + 1e-30)
        else:
            s = MOMENT_SCALE[name]
        km, kv = _jax.random.split(_jax.random.fold_in(key, i + 1))
        out[name] = w
        out["m_" + name] = s * _jax.random.normal(km, w.shape, _jnp.float32)
        out["v_" + name] = (s * s) * _jax.random.uniform(kv, w.shape, _jnp.float32, 0.5, 1.5)
    if N_MICROBATCH > 1:
        for name, axis in PER_EXAMPLE_BATCH_AXIS.items():
            out[name] = _to_microbatches(out[name], axis)
    return {'x': out['x'], 'p': out['p'], 'norm_mix_g': out['norm_mix_g'], 'norm_mlp_g': out['norm_mlp_g'], 'norm_ple_g': out['norm_ple_g'], 'norm_f_g': out['norm_f_g'], 'w_in_rec': out['w_in_rec'], 'conv_a_w': out['conv_a_w'], 'conv_a_b': out['conv_a_b'], 'ln_a_g': out['ln_a_g'], 'ln_a_b': out['ln_a_b'], 'conv_b_w': out['conv_b_w'], 'conv_b_b': out['conv_b_b'], 'w_rg_a': out['w_rg_a'], 'b_rg_a': out['b_rg_a'], 'w_rg_x': out['w_rg_x'], 'b_rg_x': out['b_rg_x'], 'rg_lambda': out['rg_lambda'], 'w_out_rec': out['w_out_rec'], 'w_qkv': out['w_qkv'], 'w_o_attn': out['w_o_attn'], 'w_mlp_up': out['w_mlp_up'], 'w_mlp_down': out['w_mlp_down'], 'w_ple_proj': out['w_ple_proj'], 'w_ple_gate': out['w_ple_gate'], 'loss_target': out['loss_target'], 'm_norm_mix_g': out['m_norm_mix_g'], 'm_norm_mlp_g': out['m_norm_mlp_g'], 'm_norm_ple_g': out['m_norm_ple_g'], 'm_norm_f_g': out['m_norm_f_g'], 'm_w_in_rec': out['m_w_in_rec'], 'm_conv_a_w': out['m_conv_a_w'], 'm_conv_a_b': out['m_conv_a_b'], 'm_ln_a_g': out['m_ln_a_g'], 'm_ln_a_b': out['m_ln_a_b'], 'm_conv_b_w': out['m_conv_b_w'], 'm_conv_b_b': out['m_conv_b_b'], 'm_w_rg_a': out['m_w_rg_a'], 'm_b_rg_a': out['m_b_rg_a'], 'm_w_rg_x': out['m_w_rg_x'], 'm_b_rg_x': out['m_b_rg_x'], 'm_rg_lambda': out['m_rg_lambda'], 'm_w_out_rec': out['m_w_out_rec'], 'm_w_qkv': out['m_w_qkv'], 'm_w_o_attn': out['m_w_o_attn'], 'm_w_mlp_up': out['m_w_mlp_up'], 'm_w_mlp_down': out['m_w_mlp_down'], 'm_w_ple_proj': out['m_w_ple_proj'], 'm_w_ple_gate': out['m_w_ple_gate'], 'v_norm_mix_g': out['v_norm_mix_g'], 'v_norm_mlp_g': out['v_norm_mlp_g'], 'v_norm_ple_g': out['v_norm_ple_g'], 'v_norm_f_g': out['v_norm_f_g'], 'v_w_in_rec': out['v_w_in_rec'], 'v_conv_a_w': out['v_conv_a_w'], 'v_conv_a_b': out['v_conv_a_b'], 'v_ln_a_g': out['v_ln_a_g'], 'v_ln_a_b': out['v_ln_a_b'], 'v_conv_b_w': out['v_conv_b_w'], 'v_conv_b_b': out['v_conv_b_b'], 'v_w_rg_a': out['v_w_rg_a'], 'v_b_rg_a': out['v_b_rg_a'], 'v_w_rg_x': out['v_w_rg_x'], 'v_b_rg_x': out['v_b_rg_x'], 'v_rg_lambda': out['v_rg_lambda'], 'v_w_out_rec': out['v_w_out_rec'], 'v_w_qkv': out['v_w_qkv'], 'v_w_o_attn': out['v_w_o_attn'], 'v_w_mlp_up': out['v_w_mlp_up'], 'v_w_mlp_down': out['v_w_mlp_down'], 'v_w_ple_proj': out['v_w_ple_proj'], 'v_w_ple_gate': out['v_w_ple_gate']}


def _loss(weights, diff, rest, loss_target):
    with _jax.named_scope("forward"):
        args = {**rest, TWIN_DIFF_INPUT: diff, **{k: w.astype(_WEIGHT_DTYPES[k]) for k, w in weights.items()}}
        y = _forward(args)
    with _jax.named_scope("loss_head"):
        err = _jnp.square(y.astype(_jnp.float32) - loss_target)
        return 0.5 * _jnp.sum(_jnp.mean(err, axis=-1)) if err.ndim else 0.5 * err


def _adamw(w, g, m, v):
    m = ADAM_B1 * m + (1.0 - ADAM_B1) * g
    v = ADAM_B2 * v + (1.0 - ADAM_B2) * _jnp.square(g)
    m_hat = m / (1.0 - ADAM_B1 ** ADAM_STEP)
    v_hat = v / (1.0 - ADAM_B2 ** ADAM_STEP)
    delta = -ADAM_LR * (m_hat / (_jnp.sqrt(v_hat) + ADAM_EPS) + ADAM_WD * w)
    return delta, m, v


def reference(x, p, norm_mix_g, norm_mlp_g, norm_ple_g, norm_f_g, w_in_rec, conv_a_w, conv_a_b, ln_a_g, ln_a_b, conv_b_w, conv_b_b, w_rg_a, b_rg_a, w_rg_x, b_rg_x, rg_lambda, w_out_rec, w_qkv, w_o_attn, w_mlp_up, w_mlp_down, w_ple_proj, w_ple_gate, loss_target, m_norm_mix_g, m_norm_mlp_g, m_norm_ple_g, m_norm_f_g, m_w_in_rec, m_conv_a_w, m_conv_a_b, m_ln_a_g, m_ln_a_b, m_conv_b_w, m_conv_b_b, m_w_rg_a, m_b_rg_a, m_w_rg_x, m_b_rg_x, m_rg_lambda, m_w_out_rec, m_w_qkv, m_w_o_attn, m_w_mlp_up, m_w_mlp_down, m_w_ple_proj, m_w_ple_gate, v_norm_mix_g, v_norm_mlp_g, v_norm_ple_g, v_norm_f_g, v_w_in_rec, v_conv_a_w, v_conv_a_b, v_ln_a_g, v_ln_a_b, v_conv_b_w, v_conv_b_b, v_w_rg_a, v_b_rg_a, v_w_rg_x, v_b_rg_x, v_rg_lambda, v_w_out_rec, v_w_qkv, v_w_o_attn, v_w_mlp_up, v_w_mlp_down, v_w_ple_proj, v_w_ple_gate):
    given = dict(x=x, p=p, norm_mix_g=norm_mix_g, norm_mlp_g=norm_mlp_g, norm_ple_g=norm_ple_g, norm_f_g=norm_f_g, w_in_rec=w_in_rec, conv_a_w=conv_a_w, conv_a_b=conv_a_b, ln_a_g=ln_a_g, ln_a_b=ln_a_b, conv_b_w=conv_b_w, conv_b_b=conv_b_b, w_rg_a=w_rg_a, b_rg_a=b_rg_a, w_rg_x=w_rg_x, b_rg_x=b_rg_x, rg_lambda=rg_lambda, w_out_rec=w_out_rec, w_qkv=w_qkv, w_o_attn=w_o_attn, w_mlp_up=w_mlp_up, w_mlp_down=w_mlp_down, w_ple_proj=w_ple_proj, w_ple_gate=w_ple_gate, loss_target=loss_target, m_norm_mix_g=m_norm_mix_g, m_norm_mlp_g=m_norm_mlp_g, m_norm_ple_g=m_norm_ple_g, m_norm_f_g=m_norm_f_g, m_w_in_rec=m_w_in_rec, m_conv_a_w=m_conv_a_w, m_conv_a_b=m_conv_a_b, m_ln_a_g=m_ln_a_g, m_ln_a_b=m_ln_a_b, m_conv_b_w=m_conv_b_w, m_conv_b_b=m_conv_b_b, m_w_rg_a=m_w_rg_a, m_b_rg_a=m_b_rg_a, m_w_rg_x=m_w_rg_x, m_b_rg_x=m_b_rg_x, m_rg_lambda=m_rg_lambda, m_w_out_rec=m_w_out_rec, m_w_qkv=m_w_qkv, m_w_o_attn=m_w_o_attn, m_w_mlp_up=m_w_mlp_up, m_w_mlp_down=m_w_mlp_down, m_w_ple_proj=m_w_ple_proj, m_w_ple_gate=m_w_ple_gate, v_norm_mix_g=v_norm_mix_g, v_norm_mlp_g=v_norm_mlp_g, v_norm_ple_g=v_norm_ple_g, v_norm_f_g=v_norm_f_g, v_w_in_rec=v_w_in_rec, v_conv_a_w=v_conv_a_w, v_conv_a_b=v_conv_a_b, v_ln_a_g=v_ln_a_g, v_ln_a_b=v_ln_a_b, v_conv_b_w=v_conv_b_w, v_conv_b_b=v_conv_b_b, v_w_rg_a=v_w_rg_a, v_b_rg_a=v_b_rg_a, v_w_rg_x=v_w_rg_x, v_b_rg_x=v_b_rg_x, v_rg_lambda=v_rg_lambda, v_w_out_rec=v_w_out_rec, v_w_qkv=v_w_qkv, v_w_o_attn=v_w_o_attn, v_w_mlp_up=v_w_mlp_up, v_w_mlp_down=v_w_mlp_down, v_w_ple_proj=v_w_ple_proj, v_w_ple_gate=v_w_ple_gate)
    weights = {n: given[n] for n in TWIN_WEIGHTS}
    shared = {n: given[n] for n in SHARED_INPUTS}
    per_example = {n: given[n] for n in ['x', 'p']}
    grad_fn = _jax.value_and_grad(_loss, argnums=(0, 1))

    def one_microbatch(ex, loss_target):
        ex = dict(ex)
        diff = ex.pop(TWIN_DIFF_INPUT)
        return grad_fn(weights, diff, {**shared, **ex}, loss_target)

    if N_MICROBATCH == 1:
        loss, (grad_w, grad_x) = one_microbatch(per_example, given["loss_target"])
    else:
        def body(carry, xs):
            loss_sum, grad_sum = carry
            l_k, (gw_k, gx_k) = one_microbatch(xs[0], xs[1])
            with _jax.named_scope("update"):
                return (loss_sum + l_k, _jax.tree.map(_jnp.add, grad_sum, gw_k)), gx_k

        init = (_jnp.zeros((), _jnp.float32), _jax.tree.map(_jnp.zeros_like, weights))
        (loss, grad_w), grad_x = _jax.lax.scan(body, init, (per_example, given["loss_target"]))
    with _jax.named_scope("update"):
        delta_w, new_m, new_v = {}, {}, {}
        for n in TWIN_WEIGHTS:
            delta_w[n], new_m[n], new_v[n] = _adamw(weights[n], grad_w[n], given["m_" + n], given["v_" + n])
    return (loss, grad_x, *[grad_w[n] for n in TWIN_WEIGHTS], *[delta_w[n] for n in TWIN_WEIGHTS],
            *[new_m[n] for n in TWIN_WEIGHTS], *[new_v[n] for n in TWIN_WEIGHTS])
```

```python
import math

import jax
import jax.numpy as jnp
from jax import lax
from jax.experimental import pallas as pl
from jax.experimental.pallas import tpu as pltpu

F32 = jnp.float32
BF = jnp.bfloat16
MESH = pl.DeviceIdType.MESH

VMEM_LIMIT_BYTES = 56 * 1024 * 1024
LANES = 128
SUBLANES = 8
PACK_COLS = 512
N_CHIPS = 4
N_DEV = 8

EPS = 1e-6
SB_HEADS = 16
RG_HEADS = 8
RG_C = 8.0
ADAM_LR = 0.001
ADAM_B1 = 0.9
ADAM_B2 = 0.999
ADAM_EPS = 1e-08
ADAM_WD = 0.01
ADAM_STEP = 10

BIG = ("w_in_rec", "w_out_rec", "w_qkv", "w_o_attn", "w_mlp_up", "w_mlp_down", "w_ple_proj", "w_ple_gate")
COL_SHARDED = ("w_in_rec", "w_qkv", "w_mlp_up", "w_ple_proj")
SMALL = ("norm_mix_g", "norm_mlp_g", "norm_ple_g", "norm_f_g", "conv_a_w", "conv_a_b", "ln_a_g", "ln_a_b",
         "conv_b_w", "conv_b_b", "w_rg_a", "b_rg_a", "w_rg_x", "b_rg_x", "rg_lambda")
WEIGHTS = ("norm_mix_g", "norm_mlp_g", "norm_ple_g", "norm_f_g", "w_in_rec", "conv_a_w", "conv_a_b", "ln_a_g",
           "ln_a_b", "conv_b_w", "conv_b_b", "w_rg_a", "b_rg_a", "w_rg_x", "b_rg_x", "rg_lambda", "w_out_rec",
           "w_qkv", "w_o_attn", "w_mlp_up", "w_mlp_down", "w_ple_proj", "w_ple_gate")


def _params(sem):
    return pltpu.CompilerParams(dimension_semantics=sem, vmem_limit_bytes=VMEM_LIMIT_BYTES)


def _pow2_floor(n):
    return 1 << (int(n).bit_length() - 1)


def _sig(x):
    return 0.5 * (jnp.tanh(0.5 * x) + 1.0)


def _softplus(x):
    return jnp.maximum(x, 0.0) + jnp.log(1.0 + jnp.exp(-jnp.maximum(x, -x)))


def _rms(x, g):
    return x * lax.rsqrt(jnp.mean(x * x, axis=-1, keepdims=True) + EPS) * g


def _neg_expm1(x):
    series = -x * (1.0 + 0.5 * x * (1.0 + x * (1.0 / 3.0) * (1.0 + 0.25 * x)))
    return jnp.where(x > -1e-2, series, 1.0 - jnp.exp(x))


def f_norm(h, g):
    return (_rms(h, g),)


def f_add_norm(h, mix, g):
    h1 = h + mix
    return h1, _rms(h1, g)


def f_ple(h, pp, gpre):
    return (h + pp * _sig(gpre),)


def f_relu2(u):
    r = jnp.maximum(u, 0.0)
    return (r * r,)


def f_glu(a, b):
    return (a * _sig(b),)


def f_ln_silu(x, g, b):
    mu = jnp.mean(x, axis=-1, keepdims=True)
    xc = x - mu
    var = jnp.mean(xc * xc, axis=-1, keepdims=True)
    y = xc * lax.rsqrt(var + EPS) * g + b
    return (y * _sig(y),)


def f_gelu_gate(hs, gr):
    inner = math.sqrt(2.0 / math.pi) * (gr + 0.044715 * gr * gr * gr)
    return (hs * (0.5 * gr * (1.0 + jnp.tanh(inner))),)


def f_gates(ra, ix, xc, b_a, b_x, lam):
    r = _sig(ra + b_a)
    i = _sig(ix + b_x)
    log_a = -RG_C * r * _softplus(-lam)
    a = jnp.exp(log_a)
    mult = jnp.sqrt(_neg_expm1(2.0 * log_a))
    return a, mult * (i * xc)


class Cols:
    def __init__(self, arr, blk, width):
        self.arr, self.blk, self.width = arr, blk, width
        self.shape = (arr.shape[0], width)
        self.dtype = arr.dtype


def _row_spec(a, t):
    if isinstance(a, Cols):
        blk = a.blk
        return pl.BlockSpec((t, a.width), lambda i: (i, blk))
    nd = len(a.shape)
    return pl.BlockSpec((t,) + tuple(a.shape[1:]), lambda i: (i,) + (0,) * (nd - 1))


def _full_spec(a):
    nd = len(a.shape)
    return pl.BlockSpec(tuple(a.shape), lambda i: (0,) * nd)


def _arr(a):
    return a.arr if isinstance(a, Cols) else a


def _row_tile(shapes):
    s = shapes[0][0]
    widest = max(int(math.prod(sh[1:])) for sh in shapes)
    t = _pow2_floor(max(16, (1 << 18) // widest))
    t = min(t, s)
    assert s % t == 0
    return t


def _rows_fwd(name, fn, rows, params, outs):
    s = rows[0].shape[0]
    t = _row_tile([r.shape for r in rows] + [(s,) + tuple(o[0]) for o in outs])
    nin = len(rows) + len(params)

    def body(*refs):
        vals = [r[...].astype(F32) for r in refs[:nin]]
        res = fn(*vals)
        for o_ref, v in zip(refs[nin:], res):
            o_ref[...] = v.astype(o_ref.dtype)

    out_shape = [jax.ShapeDtypeStruct((s,) + tuple(o[0]), o[1]) for o in outs]
    res = pl.pallas_call(
        body, grid=(s // t,), name=name,
        in_specs=[_row_spec(r, t) for r in rows] + [_full_spec(p) for p in params],
        out_specs=[_row_spec(o, t) for o in out_shape], out_shape=out_shape,
        compiler_params=_params(("parallel",)),
    )(*[_arr(r) for r in rows], *params)
    return res


def _rows_bwd(name, fn, rows, params, cots, out_dtypes, addend=None):
    s = rows[0].shape[0]
    nr, npar, nc = len(rows), len(params), len(cots)
    extra = [addend] if addend is not None else []
    t = _row_tile([r.shape for r in rows] + [c.shape for c in cots])

    def body(*refs):
        rs = [r[...].astype(F32) for r in refs[:nr]]
        ps = [r[...].astype(F32) for r in refs[nr:nr + npar]]
        cs = tuple(r[...].astype(F32) for r in refs[nr + npar:nr + npar + nc])
        k = nr + npar + nc
        ad = refs[k][...].astype(F32) if extra else None
        k += len(extra)
        grow = refs[k:k + nr]
        gpar = refs[k + nr:]
        _, vjp = jax.vjp(fn, *rs, *ps)
        g = vjp(cs)
        for j in range(nr):
            val = g[j]
            if j == 0 and ad is not None:
                val = val + ad
            grow[j][...] = val.astype(grow[j].dtype)
        first = pl.program_id(0) == 0
        for j in range(npar):
            @pl.when(first)
            def _(j=j):
                gpar[j][...] = jnp.zeros_like(gpar[j])
            gpar[j][...] += g[nr + j]

    out_shape = ([jax.ShapeDtypeStruct(tuple(r.shape), dt) for r, dt in zip(rows, out_dtypes)]
                 + [jax.ShapeDtypeStruct(tuple(p.shape), F32) for p in params])
    res = pl.pallas_call(
        body, grid=(s // t,), name=name,
        in_specs=([_row_spec(r, t) for r in rows] + [_full_spec(p) for p in params]
                  + [_row_spec(c, t) for c in cots] + [_row_spec(a, t) for a in extra]),
        out_specs=([_row_spec(o, t) for o in out_shape[:nr]] + [_full_spec(o) for o in out_shape[nr:]]),
        out_shape=out_shape,
        compiler_params=_params(("arbitrary",)),
    )(*[_arr(r) for r in rows], *params, *[_arr(c) for c in cots], *extra)
    return list(res[:nr]), list(res[nr:])


def _loss_and_grad(name, h, tgt, g):
    s, d = h.shape
    t = _row_tile([h.shape])

    def body(h_ref, t_ref, g_ref, loss_ref, dh_ref, dg_ref):
        tg = t_ref[...]

        def f(hh, gg):
            e = _rms(hh, gg) - tg
            return 0.5 * jnp.mean(e * e, axis=-1, keepdims=True)

        val, vjp = jax.vjp(f, h_ref[...], g_ref[...])
        dh, dg = vjp(jnp.ones_like(val))
        dh_ref[...] = dh

        @pl.when(pl.program_id(0) == 0)
        def _():
            loss_ref[...] = jnp.zeros_like(loss_ref)
            dg_ref[...] = jnp.zeros_like(dg_ref)
        loss_ref[...] += jnp.broadcast_to(jnp.sum(val, axis=0, keepdims=True), loss_ref.shape)
        dg_ref[...] += dg

    return pl.pallas_call(
        body, grid=(s // t,), name=name,
        in_specs=[_row_spec(h, t), _row_spec(tgt, t), _full_spec(g)],
        out_specs=[pl.BlockSpec((1, LANES), lambda i: (0, 0)), _row_spec(h, t), _full_spec(g)],
        out_shape=[jax.ShapeDtypeStruct((1, LANES), F32), jax.ShapeDtypeStruct((s, d), F32),
                   jax.ShapeDtypeStruct(tuple(g.shape), F32)],
        compiler_params=_params(("arbitrary",)),
    )(h, tgt, g)


def _col_tile(nc):
    for t in (1024, 768, 512):
        if nc >= t and nc % t == 0:
            return t
    return nc


def _mm(name, a, b, grid, in_specs, out_spec, out_shape, dims, acc_shape):
    nsteps = grid[2]

    def body(a_ref, b_ref, o_ref, acc_ref):
        k = pl.program_id(2)

        @pl.when(k == 0)
        def _():
            acc_ref[...] = jnp.zeros_like(acc_ref)
        acc_ref[...] += lax.dot_general(a_ref[...].astype(BF), b_ref[...].astype(BF), dims,
                                        preferred_element_type=F32)

        @pl.when(k == nsteps - 1)
        def _():
            o_ref[...] = acc_ref[...].astype(o_ref.dtype)

    return pl.pallas_call(
        body, grid=grid, name=name, in_specs=in_specs, out_specs=out_spec, out_shape=out_shape,
        scratch_shapes=[pltpu.VMEM(acc_shape, F32)],
        compiler_params=_params(("parallel", "parallel", "arbitrary")),
    )(a, b)


def _mm_nn(name, a, w, out_dtype):
    m, k = a.shape
    g, k2, nc = w.shape
    assert k == k2
    tm, tk, tn = min(m, 1024), min(k, 1024), _col_tile(nc)
    r = nc // tn
    return _mm(name, a, w, (m // tm, g * r, k // tk),
               [pl.BlockSpec((tm, tk), lambda i, j, l: (i, l)),
                pl.BlockSpec((None, tk, tn), lambda i, j, l: (j // r, l, j % r))],
               pl.BlockSpec((tm, tn), lambda i, j, l: (i, j)),
               jax.ShapeDtypeStruct((m, g * nc), out_dtype), (((1,), (0,)), ((), ())), (tm, tn))


def _mm_nt(name, a, w, out_dtype):
    m, n = a.shape
    g, k, nc = w.shape
    assert n == g * nc
    tm, tk, tn = min(m, 1024), min(k, 1024), _col_tile(nc)
    r = nc // tn
    return _mm(name, a, w, (m // tm, k // tk, g * r),
               [pl.BlockSpec((tm, tn), lambda i, j, l: (i, l)),
                pl.BlockSpec((None, tk, tn), lambda i, j, l: (l // r, j, l % r))],
               pl.BlockSpec((tm, tk), lambda i, j, l: (i, j)),
               jax.ShapeDtypeStruct((m, k), out_dtype), (((1,), (1,)), ((), ())), (tm, tk))


def _mm_tn(name, a, b, g):
    m, k = a.shape
    m2, n = b.shape
    assert m == m2 and n % g == 0
    nc = n // g
    tm, tk, tn = min(m, 1024), min(k, 1024), _col_tile(nc)
    r = nc // tn
    return _mm(name, a, b, (k // tk, g * r, m // tm),
               [pl.BlockSpec((tm, tk), lambda i, j, l: (l, i)),
                pl.BlockSpec((tm, tn), lambda i, j, l: (l, j))],
               pl.BlockSpec((None, tk, tn), lambda i, j, l: (j // r, i, j % r)),
               jax.ShapeDtypeStruct((g, k, nc), F32), (((0,), (0,)), ((), ())), (tk, tn))


def _bd_nn(name, x, w):
    s, c = x.shape
    nh, hd, _ = w.shape
    tm = min(s, 1024)

    def body(x_ref, w_ref, o_ref):
        o_ref[...] = jnp.dot(x_ref[...].astype(BF), w_ref[...].astype(BF), preferred_element_type=F32)

    return pl.pallas_call(
        body, grid=(s // tm, nh), name=name,
        in_specs=[pl.BlockSpec((tm, hd), lambda i, h: (i, h)), pl.BlockSpec((None, hd, hd), lambda i, h: (h, 0, 0))],
        out_specs=pl.BlockSpec((tm, hd), lambda i, h: (i, h)),
        out_shape=jax.ShapeDtypeStruct((s, c), F32),
        compiler_params=_params(("parallel", "parallel")),
    )(x, w)


def _bd_nt2(name, add, dy1, w1, dy2, w2):
    s, c = add.shape
    nh, hd, _ = w1.shape
    tm = min(s, 1024)
    nt = (((1,), (1,)), ((), ()))

    def body(a_ref, d1_ref, w1_ref, d2_ref, w2_ref, o_ref):
        o_ref[...] = (a_ref[...]
                      + lax.dot_general(d1_ref[...].astype(BF), w1_ref[...].astype(BF), nt, preferred_element_type=F32)
                      + lax.dot_general(d2_ref[...].astype(BF), w2_ref[...].astype(BF), nt, preferred_element_type=F32))

    row = pl.BlockSpec((tm, hd), lambda i, h: (i, h))
    wsp = pl.BlockSpec((None, hd, hd), lambda i, h: (h, 0, 0))
    return pl.pallas_call(
        body, grid=(s // tm, nh), name=name, in_specs=[row, row, wsp, row, wsp], out_specs=row,
        out_shape=jax.ShapeDtypeStruct((s, c), F32), compiler_params=_params(("parallel", "parallel")),
    )(add, dy1, w1, dy2, w2)


def _bd_tn(name, x, dy, nh):
    s, c = x.shape
    hd = c // nh
    tm = min(s, 1024)
    tn = (((0,), (0,)), ((), ()))

    def body(x_ref, d_ref, o_ref):
        @pl.when(pl.program_id(1) == 0)
        def _():
            o_ref[...] = jnp.zeros_like(o_ref)
        o_ref[...] += lax.dot_general(x_ref[...].astype(BF), d_ref[...].astype(BF), tn, preferred_element_type=F32)

    row = pl.BlockSpec((tm, hd), lambda h, i: (i, h))
    return pl.pallas_call(
        body, grid=(nh, s // tm), name=name, in_specs=[row, row],
        out_specs=pl.BlockSpec((None, hd, hd), lambda h, i: (h, 0, 0)),
        out_shape=jax.ShapeDtypeStruct((nh, hd, hd), F32), compiler_params=_params(("parallel", "arbitrary")),
    )(x, dy)


CONV_HALO = 32
CONV_SUB = 16


def _conv3(name, x3, w3, b2, causal):
    s, sl, hd = x3.shape
    kw = w3.shape[0]
    tc = min(s, 256)
    nchunks = s // tc
    per = tc // CONV_HALO
    nhalo = s // CONV_HALO
    assert kw - 1 <= CONV_HALO and tc % CONV_HALO == 0 and tc % CONV_SUB == 0
    has_bias = b2 is not None

    def body(*refs):
        if has_bias:
            cur_ref, halo_ref, w_ref, b_ref, y_ref, win_ref = refs
        else:
            cur_ref, halo_ref, w_ref, y_ref, win_ref = refs
        i = pl.program_id(0)
        if causal:
            win_ref[0:CONV_HALO] = jnp.where(i > 0, halo_ref[...], 0.0)
            win_ref[CONV_HALO:CONV_HALO + tc] = cur_ref[...]
        else:
            win_ref[0:tc] = cur_ref[...]
            win_ref[tc:tc + CONV_HALO] = jnp.where(i < nchunks - 1, halo_ref[...], 0.0)

        def sub_step(j, carry):
            t0 = pl.multiple_of(j * CONV_SUB, CONV_SUB)
            if has_bias:
                acc = jnp.broadcast_to(b_ref[...], (CONV_SUB, sl, hd))
            else:
                acc = jnp.zeros((CONV_SUB, sl, hd), F32)
            for k in range(kw):
                off = CONV_HALO - (kw - 1) + k if causal else kw - 1 - k
                acc = acc + w_ref[k] * win_ref[pl.ds(t0 + off, CONV_SUB)]
            y_ref[pl.ds(t0, CONV_SUB)] = acc
            return carry

        lax.fori_loop(0, tc // CONV_SUB, sub_step, 0)

    if causal:
        halo_map = lambda i: (jnp.maximum(i * per - 1, 0), 0, 0)
    else:
        halo_map = lambda i: (jnp.minimum((i + 1) * per, nhalo - 1), 0, 0)
    chunk = pl.BlockSpec((tc, sl, hd), lambda i: (i, 0, 0))
    in_specs = [chunk, pl.BlockSpec((CONV_HALO, sl, hd), halo_map), pl.BlockSpec((kw, sl, hd), lambda i: (0, 0, 0))]
    args = [x3, x3, w3]
    if has_bias:
        in_specs.append(pl.BlockSpec((sl, hd), lambda i: (0, 0)))
        args.append(b2)
    return pl.pallas_call(
        body, grid=(nchunks,), name=name, in_specs=in_specs, out_specs=chunk,
        out_shape=jax.ShapeDtypeStruct((s, sl, hd), F32),
        scratch_shapes=[pltpu.VMEM((tc + CONV_HALO, sl, hd), F32)],
        compiler_params=_params(("parallel",)),
    )(*args)


def _conv3_bwd_w(name, dy3, x3, kw):
    s, sl, hd = x3.shape
    tc = min(s, 256)
    per = tc // CONV_HALO

    def body(dy_ref, cur_ref, halo_ref, dw_ref, db_ref, win_ref):
        i = pl.program_id(0)

        @pl.when(i == 0)
        def _():
            dw_ref[...] = jnp.zeros_like(dw_ref)
            db_ref[...] = jnp.zeros_like(db_ref)
        win_ref[0:CONV_HALO] = jnp.where(i > 0, halo_ref[...], 0.0)
        win_ref[CONV_HALO:CONV_HALO + tc] = cur_ref[...]

        def sub_step(j, carry):
            t0 = pl.multiple_of(j * SUBLANES, SUBLANES)
            dy = dy_ref[pl.ds(t0, SUBLANES)]
            new = [carry[k] + jnp.sum(dy * win_ref[pl.ds(t0 + CONV_HALO - (kw - 1) + k, SUBLANES)], axis=0)
                   for k in range(kw)]
            new.append(carry[kw] + jnp.sum(dy, axis=0))
            return tuple(new)

        zero = jnp.zeros((sl, hd), F32)
        res = lax.fori_loop(0, tc // SUBLANES, sub_step, tuple(zero for _ in range(kw + 1)))
        for k in range(kw):
            dw_ref[k] += res[k]
        db_ref[...] += res[kw]

    chunk = pl.BlockSpec((tc, sl, hd), lambda i: (i, 0, 0))
    return pl.pallas_call(
        body, grid=(s // tc,), name=name,
        in_specs=[chunk, chunk, pl.BlockSpec((CONV_HALO, sl, hd), lambda i: (jnp.maximum(i * per - 1, 0), 0, 0))],
        out_specs=[pl.BlockSpec((kw, sl, hd), lambda i: (0, 0, 0)), pl.BlockSpec((sl, hd), lambda i: (0, 0))],
        out_shape=[jax.ShapeDtypeStruct((kw, sl, hd), F32), jax.ShapeDtypeStruct((sl, hd), F32)],
        scratch_shapes=[pltpu.VMEM((tc + CONV_HALO, sl, hd), F32)],
        compiler_params=_params(("arbitrary",)),
    )(dy3, x3, x3)


def _scan_fwd(name, a3, u3):
    s, sl, hd = a3.shape
    tc = min(s, 512)

    def body(a_ref, u_ref, h_ref, carry_ref):
        @pl.when(pl.program_id(0) == 0)
        def _():
            carry_ref[...] = jnp.zeros_like(carry_ref)

        def step(t, h):
            h = a_ref[t] * h + u_ref[t]
            h_ref[t] = h
            return h

        carry_ref[...] = lax.fori_loop(0, tc, step, carry_ref[...], unroll=8)

    chunk = pl.BlockSpec((tc, sl, hd), lambda i: (i, 0, 0))
    return pl.pallas_call(
        body, grid=(s // tc,), name=name, in_specs=[chunk, chunk], out_specs=chunk,
        out_shape=jax.ShapeDtypeStruct((s, sl, hd), F32), scratch_shapes=[pltpu.VMEM((sl, hd), F32)],
        compiler_params=_params(("arbitrary",)),
    )(a3, u3)


def _scan_bwd(name, a3, gh3, hprev3):
    s, sl, hd = a3.shape
    tc = min(s, 512)
    n = s // tc

    def body(a_ref, g_ref, hp_ref, da_ref, du_ref, carry_ref):
        @pl.when(pl.program_id(0) == 0)
        def _():
            carry_ref[...] = jnp.zeros_like(carry_ref)

        def step(j, c):
            t = tc - 1 - j
            lam = g_ref[t] + c
            du_ref[t] = lam
            da_ref[t] = lam * hp_ref[t]
            return a_ref[t] * lam

        carry_ref[...] = lax.fori_loop(0, tc, step, carry_ref[...], unroll=8)

    chunk = pl.BlockSpec((tc, sl, hd), lambda i: (n - 1 - i, 0, 0))
    shp = jax.ShapeDtypeStruct((s, sl, hd), F32)
    return pl.pallas_call(
        body, grid=(n,), name=name, in_specs=[chunk, chunk, chunk], out_specs=[chunk, chunk],
        out_shape=[shp, shp], scratch_shapes=[pltpu.VMEM((sl, hd), F32)],
        compiler_params=_params(("arbitrary",)),
    )(a3, gh3, hprev3)


ATT_BLOCK = 128
NT_DIMS = (((1,), (1,)), ((), ()))
TN_DIMS = (((0,), (0,)), ((), ()))


def _hilo_dot(x, u):
    hi = x.astype(BF)
    lo = (x - hi.astype(F32)).astype(BF)
    return jnp.dot(hi, u, preferred_element_type=F32) + jnp.dot(lo, u, preferred_element_type=F32)


def _attn_fwd(name, qkv, nh):
    s, d3 = qkv.shape
    d = d3 // 3
    dh = d // nh
    tb = min(s, ATT_BLOCK)
    scale = 1.0 / math.sqrt(dh)

    def body(q_ref, k_ref, v_ref, o_ref, lam_ref):
        i = pl.program_id(1)
        q = q_ref[...]
        row = lax.broadcasted_iota(jnp.int32, (tb, tb), 0)
        col = lax.broadcasted_iota(jnp.int32, (tb, tb), 1)
        after = (row > col).astype(BF)
        qpos = i * tb + row

        def step(jj, carry):
            acc, tail_carry = carry
            b = i - jj
            off = pl.multiple_of(b * tb, tb)
            kt = k_ref[pl.ds(off, tb), :]
            vt = v_ref[pl.ds(off, tb), :]
            z = lax.dot_general(q, kt, NT_DIMS, preferred_element_type=F32) * scale
            mask = (off + col) < qpos
            lraw = -_softplus(z)
            lm = jnp.where(mask, lraw, 0.0)
            tail = _hilo_dot(lm, after) + tail_carry
            w = jnp.where(mask, jnp.exp(z + lraw + tail), 0.0)
            acc = acc + jnp.dot(w.astype(BF), vt, preferred_element_type=F32)
            return acc, tail_carry + jnp.sum(lm, axis=1, keepdims=True)

        acc, total = lax.fori_loop(0, i + 1, step, (jnp.zeros((tb, dh), F32), jnp.zeros((tb, 1), F32)))
        o_ref[...] = acc.astype(o_ref.dtype)
        lam_ref[...] = total

    return pl.pallas_call(
        body, grid=(nh, s // tb), name=name,
        in_specs=[pl.BlockSpec((tb, dh), lambda h, i: (i, h)),
                  pl.BlockSpec((s, dh), lambda h, i: (0, nh + h)),
                  pl.BlockSpec((s, dh), lambda h, i: (0, 2 * nh + h))],
        out_specs=[pl.BlockSpec((tb, dh), lambda h, i: (i, h)), pl.BlockSpec((None, tb, 1), lambda h, i: (h, i, 0))],
        out_shape=[jax.ShapeDtypeStruct((s, d), BF), jax.ShapeDtypeStruct((nh, s, 1), F32)],
        compiler_params=_params(("parallel", "arbitrary")),
    )(qkv, qkv, qkv)


def _attn_bwd(name, qkv, lam, do, nh):
    s, d3 = qkv.shape
    d = d3 // 3
    dh = d // nh
    tb = min(s, ATT_BLOCK)
    scale = 1.0 / math.sqrt(dh)

    def body(q_ref, k_ref, v_ref, do_ref, lam_ref, dq_ref, dk_ref, dv_ref):
        i = pl.program_id(1)

        @pl.when(i == 0)
        def _():
            dk_ref[...] = jnp.zeros_like(dk_ref)
            dv_ref[...] = jnp.zeros_like(dv_ref)
        q = q_ref[...]
        dout = do_ref[...]
        total = lam_ref[...]
        row = lax.broadcasted_iota(jnp.int32, (tb, tb), 0)
        col = lax.broadcasted_iota(jnp.int32, (tb, tb), 1)
        upto = (row <= col).astype(BF)
        before = (row < col).astype(BF)
        qpos = i * tb + row

        def step(b, carry):
            dq, l_carry, g_carry = carry
            off = pl.multiple_of(b * tb, tb)
            kt = k_ref[pl.ds(off, tb), :]
            vt = v_ref[pl.ds(off, tb), :]
            z = lax.dot_general(q, kt, NT_DIMS, preferred_element_type=F32) * scale
            mask = (off + col) < qpos
            lraw = -_softplus(z)
            lm = jnp.where(mask, lraw, 0.0)
            tail = total - (_hilo_dot(lm, upto) + l_carry)
            w = jnp.where(mask, jnp.exp(z + lraw + tail), 0.0)
            dw = lax.dot_general(dout, vt, NT_DIMS, preferred_element_type=F32)
            g = w * dw
            gsum = _hilo_dot(g, before) + g_carry
            sg = jnp.exp(z + lraw)
            dz = (jnp.where(mask, g * (1.0 - sg) - sg * gsum, 0.0) * scale).astype(BF)
            dq = dq + jnp.dot(dz, kt, preferred_element_type=F32)
            dk_ref[pl.ds(off, tb), :] += lax.dot_general(dz, q, TN_DIMS, preferred_element_type=F32)
            dv_ref[pl.ds(off, tb), :] += lax.dot_general(w.astype(BF), dout, TN_DIMS, preferred_element_type=F32)
            return (dq, l_carry + jnp.sum(lm, axis=1, keepdims=True), g_carry + jnp.sum(g, axis=1, keepdims=True))

        zero_col = jnp.zeros((tb, 1), F32)
        dq, _, _ = lax.fori_loop(0, i + 1, step, (jnp.zeros((tb, dh), F32), zero_col, zero_col))
        dq_ref[...] = dq

    blk = pl.BlockSpec((tb, dh), lambda h, i: (i, h))
    head = pl.BlockSpec((s, dh), lambda h, i: (0, h))
    shp = jax.ShapeDtypeStruct((s, d), F32)
    return pl.pallas_call(
        body, grid=(nh, s // tb), name=name,
        in_specs=[blk, pl.BlockSpec((s, dh), lambda h, i: (0, nh + h)),
                  pl.BlockSpec((s, dh), lambda h, i: (0, 2 * nh + h)), blk,
                  pl.BlockSpec((None, tb, 1), lambda h, i: (h, i, 0))],
        out_specs=[blk, head, head], out_shape=[shp, shp, shp],
        compiler_params=_params(("parallel", "arbitrary")),
    )(qkv, qkv, qkv, do, lam)


def _adamw(name, w, g, m, v):
    shape = w.shape
    c = shape[-1]
    r = int(math.prod(shape[:-1])) if len(shape) > 1 else 1
    w2, g2, m2, v2 = (a.reshape(r, c) for a in (w, g, m, v))
    t = min(r, max(SUBLANES, _pow2_floor((1 << 19) // (4 * c))))
    if r % t:
        t = r
    assert r * c * 4 <= (1 << 22) or t < r

    def body(w_ref, g_ref, m_ref, v_ref, d_ref, nm_ref, nv_ref):
        gg = g_ref[...]
        nm = ADAM_B1 * m_ref[...] + (1.0 - ADAM_B1) * gg
        nv = ADAM_B2 * v_ref[...] + (1.0 - ADAM_B2) * (gg * gg)
        m_hat = nm / (1.0 - ADAM_B1 ** ADAM_STEP)
        v_hat = nv / (1.0 - ADAM_B2 ** ADAM_STEP)
        d_ref[...] = -ADAM_LR * (m_hat / (jnp.sqrt(v_hat) + ADAM_EPS) + ADAM_WD * w_ref[...])
        nm_ref[...] = nm
        nv_ref[...] = nv

    spec = pl.BlockSpec((t, c), lambda i: (i, 0))
    shp = jax.ShapeDtypeStruct((r, c), F32)
    d, nm, nv = pl.pallas_call(
        body, grid=(r // t,), name=name, in_specs=[spec] * 4, out_specs=[spec] * 3, out_shape=[shp] * 3,
        compiler_params=_params(("parallel",)),
    )(w2, g2, m2, v2)
    return d.reshape(shape), nm.reshape(shape), nv.reshape(shape)


def _add2(name, a, b):
    nb, m, c = a.shape
    t = min(m, 256)
    while m % t:
        t //= 2

    def body(a_ref, b_ref, o_ref):
        o_ref[...] = a_ref[...] + b_ref[...]

    spec = pl.BlockSpec((None, t, c), lambda k, i: (k, i, 0))
    return pl.pallas_call(body, grid=(nb, m // t), name=name, in_specs=[spec, spec], out_specs=spec,
                          out_shape=jax.ShapeDtypeStruct((nb, m, c), F32),
                          compiler_params=_params(("parallel", "parallel")))(a, b)


def _sum_chips(name, a):
    nb, m, c = a.shape
    t = min(m, 256)
    while m % t:
        t //= 2

    def body(a_ref, o_ref):
        acc = a_ref[0]
        for q in range(1, nb):
            acc = acc + a_ref[q]
        o_ref[...] = acc

    return pl.pallas_call(body, grid=(m // t,), name=name,
                          in_specs=[pl.BlockSpec((nb, t, c), lambda i: (0, i, 0))],
                          out_specs=pl.BlockSpec((t, c), lambda i: (i, 0)),
                          out_shape=jax.ShapeDtypeStruct((m, c), F32),
                          compiler_params=_params(("parallel",)))(a)


HBM_SPEC = pl.BlockSpec(memory_space=pltpu.HBM)


def _me():
    return lax.axis_index("x"), lax.axis_index("y"), lax.axis_index("c")


def _remote(src, dst, send_sem, recv_sem, dev):
    return pltpu.make_async_remote_copy(src_ref=src, dst_ref=dst, send_sem=send_sem, recv_sem=recv_sem,
                                        device_id=dev, device_id_type=MESH)


def _all_gather_shards(name, shard):
    m2, n = shard.shape
    m = m2 // 2

    def body(x_ref, out_ref, send_sems, recv_sems, local_sem):
        x, y, c = _me()
        me, sibling = (x, y, c), (x, y, 1 - c)
        chips = [(1 - x, y), (x, 1 - y), (1 - x, 1 - y)]

        def rows(px, py, pc):
            return out_ref.at[pl.ds((4 * px + 2 * py + pc) * m, m), :]

        mine = pltpu.make_async_copy(x_ref, out_ref.at[pl.ds((4 * x + 2 * y) * m, 2 * m), :], local_sem)
        mine.start()
        half = x_ref.at[pl.ds(c * m, m), :]
        first = [_remote(half, rows(x, y, c), send_sems.at[j], recv_sems.at[j], (*chip, c))
                 for j, chip in enumerate(chips)]
        for cp in first:
            cp.start()
        passed = [_remote(rows(*chip, c), rows(*chip, c), send_sems.at[3 + j], recv_sems.at[3 + j], sibling)
                  for j, chip in enumerate(chips)]
        for j, chip in enumerate(chips):
            _remote(half, rows(*chip, c), send_sems.at[j], recv_sems.at[j], me).wait_recv()
            passed[j].start()
        for j, chip in enumerate(chips):
            _remote(half, rows(*chip, 1 - c), send_sems.at[3 + j], recv_sems.at[3 + j], me).wait_recv()
        for cp in first + passed:
            cp.wait_send()
        mine.wait()

    return pl.pallas_call(
        body, name=name, out_shape=jax.ShapeDtypeStruct((N_DEV * m, n), shard.dtype),
        in_specs=[HBM_SPEC], out_specs=HBM_SPEC,
        scratch_shapes=[pltpu.SemaphoreType.DMA((6,)), pltpu.SemaphoreType.DMA((6,)), pltpu.SemaphoreType.DMA],
    )(shard)


def _pair_swap(name, src):
    nb, _, m, n = src.shape

    def body(src_ref, out_ref, send_sems, recv_sems):
        x, y, c = _me()
        cps = [_remote(src_ref.at[k, 1 - c], out_ref.at[k], send_sems.at[k], recv_sems.at[k], (x, y, 1 - c))
               for k in range(nb)]
        for cp in cps:
            cp.start()
        for cp in cps:
            cp.wait()

    return pl.pallas_call(
        body, name=name, out_shape=jax.ShapeDtypeStruct((nb, m, n), src.dtype),
        in_specs=[HBM_SPEC], out_specs=HBM_SPEC,
        scratch_shapes=[pltpu.SemaphoreType.DMA((nb,)), pltpu.SemaphoreType.DMA((nb,))],
    )(src)


def _chip_all_to_all(name, p):
    nb, m, n = p.shape

    def body(p_ref, out_ref, send_sems, recv_sems, local_sem):
        x, y, c = _me()
        j = 2 * x + y
        local = pltpu.make_async_copy(p_ref.at[j], out_ref.at[j], local_sem)
        local.start()
        peers = []
        for r in (1, 2, 3):
            px = 1 - x if r & 2 else x
            py = 1 - y if r & 1 else y
            peers.append((px, py, 2 * px + py))
        cps = [_remote(p_ref.at[q], out_ref.at[j], send_sems.at[k], recv_sems.at[k], (px, py, c))
               for k, (px, py, q) in enumerate(peers)]
        for cp in cps:
            cp.start()
        for k, (px, py, q) in enumerate(peers):
            _remote(p_ref.at[q], out_ref.at[q], send_sems.at[k], recv_sems.at[k], (px, py, c)).wait_recv()
        for cp in cps:
            cp.wait_send()
        local.wait()

    return pl.pallas_call(
        body, name=name, out_shape=jax.ShapeDtypeStruct((nb, m, n), p.dtype),
        in_specs=[HBM_SPEC], out_specs=HBM_SPEC,
        scratch_shapes=[pltpu.SemaphoreType.DMA((3,)), pltpu.SemaphoreType.DMA((3,)), pltpu.SemaphoreType.DMA],
    )(p)


def _pair_gather(name, h):
    m, n = h.shape

    def body(h_ref, out_ref, send_sem, recv_sem, local_sem):
        x, y, c = _me()
        sibling = (x, y, 1 - c)
        local = pltpu.make_async_copy(h_ref, out_ref.at[c], local_sem)
        local.start()
        cp = _remote(h_ref, out_ref.at[c], send_sem, recv_sem, sibling)
        cp.start()
        _remote(h_ref, out_ref.at[1 - c], send_sem, recv_sem, sibling).wait_recv()
        cp.wait_send()
        local.wait()

    return pl.pallas_call(
        body, name=name, out_shape=jax.ShapeDtypeStruct((2, m, n), h.dtype),
        in_specs=[HBM_SPEC], out_specs=HBM_SPEC,
        scratch_shapes=[pltpu.SemaphoreType.DMA, pltpu.SemaphoreType.DMA, pltpu.SemaphoreType.DMA],
    )(h)


def _small_allreduce(name, buf):
    r, n = buf.shape

    def body(x_ref, o_ref, land_ref, send_sems, recv_sems):
        x, y, c = _me()
        me = 4 * x + 2 * y + c
        land_ref[me] = x_ref[...]
        peers = []
        for k in range(1, N_DEV):
            px = 1 - x if k & 4 else x
            py = 1 - y if k & 2 else y
            pc = 1 - c if k & 1 else c
            peers.append((px, py, pc))
        cps = [_remote(x_ref, land_ref.at[me], send_sems.at[k], recv_sems.at[k], peer)
               for k, peer in enumerate(peers)]
        for cp in cps:
            cp.start()
        for k, (px, py, pc) in enumerate(peers):
            _remote(x_ref, land_ref.at[4 * px + 2 * py + pc], send_sems.at[k], recv_sems.at[k],
                    (px, py, pc)).wait_recv()
        for cp in cps:
            cp.wait_send()
        acc = land_ref[0]
        for q in range(1, N_DEV):
            acc = acc + land_ref[q]
        o_ref[...] = acc

    vmem = pl.BlockSpec(memory_space=pltpu.VMEM)
    return pl.pallas_call(
        body, name=name, out_shape=jax.ShapeDtypeStruct((r, n), F32), in_specs=[vmem], out_specs=vmem,
        scratch_shapes=[pltpu.VMEM((N_DEV, r, n), F32), pltpu.SemaphoreType.DMA((N_DEV - 1,)),
                        pltpu.SemaphoreType.DMA((N_DEV - 1,))],
        compiler_params=pltpu.CompilerParams(vmem_limit_bytes=VMEM_LIMIT_BYTES),
    )(buf)


def _pack(parts, total):
    lead = parts[0].shape[:-1]
    used = sum(p.shape[-1] for p in parts)
    if total > used:
        parts = list(parts) + [jnp.zeros(lead + (total - used,), parts[0].dtype)]
    flat = jnp.concatenate(parts, axis=-1)
    return flat.reshape(lead + (total // PACK_COLS, PACK_COLS))


def _small_layout(shapes):
    offs, off = {}, 0
    for name in SMALL:
        n = int(math.prod(shapes[name]))
        offs[name] = (off, n)
        off += n
    rows = -(-off // (SUBLANES * LANES)) * SUBLANES
    return offs, rows


def _pack_small(vals, offs, rows):
    parts = []
    for name in SMALL:
        off, n = offs[name]
        parts.append(vals[name].reshape(n).astype(F32) if name in vals else jnp.zeros((n,), F32))
    used = sum(p.shape[0] for p in parts)
    parts.append(jnp.zeros((rows * LANES - used,), F32))
    return jnp.concatenate(parts).reshape(rows, LANES)


def _unpack_small(buf, offs, shapes):
    flat = buf.reshape(-1)
    return {name: flat[offs[name][0]:offs[name][0] + offs[name][1]].reshape(shapes[name]) for name in SMALL}


def kernel(x, p, norm_mix_g, norm_mlp_g, norm_ple_g, norm_f_g, w_in_rec, conv_a_w, conv_a_b, ln_a_g, ln_a_b, conv_b_w, conv_b_b, w_rg_a, b_rg_a, w_rg_x, b_rg_x, rg_lambda, w_out_rec, w_qkv, w_o_attn, w_mlp_up, w_mlp_down, w_ple_proj, w_ple_gate, loss_target, m_norm_mix_g, m_norm_mlp_g, m_norm_ple_g, m_norm_f_g, m_w_in_rec, m_conv_a_w, m_conv_a_b, m_ln_a_g, m_ln_a_b, m_conv_b_w, m_conv_b_b, m_w_rg_a, m_b_rg_a, m_w_rg_x, m_b_rg_x, m_rg_lambda, m_w_out_rec, m_w_qkv, m_w_o_attn, m_w_mlp_up, m_w_mlp_down, m_w_ple_proj, m_w_ple_gate, v_norm_mix_g, v_norm_mlp_g, v_norm_ple_g, v_norm_f_g, v_w_in_rec, v_conv_a_w, v_conv_a_b, v_ln_a_g, v_ln_a_b, v_conv_b_w, v_conv_b_b, v_w_rg_a, v_b_rg_a, v_w_rg_x, v_b_rg_x, v_rg_lambda, v_w_out_rec, v_w_qkv, v_w_o_attn, v_w_mlp_up, v_w_mlp_down, v_w_ple_proj, v_w_ple_gate):
    wts = dict(norm_mix_g=norm_mix_g, norm_mlp_g=norm_mlp_g, norm_ple_g=norm_ple_g, norm_f_g=norm_f_g,
               w_in_rec=w_in_rec, conv_a_w=conv_a_w, conv_a_b=conv_a_b, ln_a_g=ln_a_g, ln_a_b=ln_a_b,
               conv_b_w=conv_b_w, conv_b_b=conv_b_b, w_rg_a=w_rg_a, b_rg_a=b_rg_a, w_rg_x=w_rg_x, b_rg_x=b_rg_x,
               rg_lambda=rg_lambda, w_out_rec=w_out_rec, w_qkv=w_qkv, w_o_attn=w_o_attn, w_mlp_up=w_mlp_up,
               w_mlp_down=w_mlp_down, w_ple_proj=w_ple_proj, w_ple_gate=w_ple_gate)
    mom = dict(norm_mix_g=m_norm_mix_g, norm_mlp_g=m_norm_mlp_g, norm_ple_g=m_norm_ple_g, norm_f_g=m_norm_f_g,
               w_in_rec=m_w_in_rec, conv_a_w=m_conv_a_w, conv_a_b=m_conv_a_b, ln_a_g=m_ln_a_g, ln_a_b=m_ln_a_b,
               conv_b_w=m_conv_b_w, conv_b_b=m_conv_b_b, w_rg_a=m_w_rg_a, b_rg_a=m_b_rg_a, w_rg_x=m_w_rg_x,
               b_rg_x=m_b_rg_x, rg_lambda=m_rg_lambda, w_out_rec=m_w_out_rec, w_qkv=m_w_qkv, w_o_attn=m_w_o_attn,
               w_mlp_up=m_w_mlp_up, w_mlp_down=m_w_mlp_down, w_ple_proj=m_w_ple_proj, w_ple_gate=m_w_ple_gate)
    var = dict(norm_mix_g=v_norm_mix_g, norm_mlp_g=v_norm_mlp_g, norm_ple_g=v_norm_ple_g, norm_f_g=v_norm_f_g,
               w_in_rec=v_w_in_rec, conv_a_w=v_conv_a_w, conv_a_b=v_conv_a_b, ln_a_g=v_ln_a_g, ln_a_b=v_ln_a_b,
               conv_b_w=v_conv_b_w, conv_b_b=v_conv_b_b, w_rg_a=v_w_rg_a, b_rg_a=v_b_rg_a, w_rg_x=v_w_rg_x,
               b_rg_x=v_b_rg_x, rg_lambda=v_rg_lambda, w_out_rec=v_w_out_rec, w_qkv=v_w_qkv, w_o_attn=v_w_o_attn,
               w_mlp_up=v_w_mlp_up, w_mlp_down=v_w_mlp_down, w_ple_proj=v_w_ple_proj, w_ple_gate=v_w_ple_gate)

    xi, yi, ci = lax.axis_index("x"), lax.axis_index("y"), lax.axis_index("c")
    chip = 2 * xi + yi
    seq, dm = x.shape[1], x.shape[2]
    depth = norm_mix_g.shape[0]
    dh2 = dm // 2
    hd = dh2 // RG_HEADS
    kw_a, kw_b = conv_a_w.shape[1], conv_b_w.shape[1]
    cshard = conv_a_w.shape[2]

    small_shapes = {n: tuple(wts[n].shape) for n in SMALL}
    small_shapes["conv_a_w"] = (conv_a_w.shape[0], kw_a, dh2)
    small_shapes["conv_b_w"] = (conv_b_w.shape[0], kw_b, dh2)
    offs, small_rows = _small_layout(small_shapes)
    south = (ci == 0).astype(F32)
    placed = {}
    for n in ("conv_a_w", "conv_b_w"):
        placed[n] = lax.dynamic_update_slice(jnp.zeros(small_shapes[n], F32), wts[n] * south,
                                             (0, 0, chip * cshard))
    conv_full = _unpack_small(_small_allreduce("small_ar_conv", _pack_small(placed, offs, small_rows)),
                              offs, small_shapes)
    conv_a_full, conv_b_full = conv_full["conv_a_w"], conv_full["conv_b_w"]

    def layer_mats(i):
        j = i // 2
        mix = ["w_in_rec", "w_out_rec"] if i % 2 == 0 else ["w_qkv", "w_o_attn"]
        names = mix + ["w_mlp_up", "w_mlp_down", "w_ple_proj", "w_ple_gate"]
        return [(n, j if n in mix else i) for n in names]

    def half_sizes(i):
        return [int(math.prod(wts[n].shape[1:])) // 2 for n, _ in layer_mats(i)]

    total = max(sum(half_sizes(i)) for i in range(depth))
    quantum = PACK_COLS * 16
    total = -(-total // quantum) * quantum
    mrows = total // PACK_COLS

    def gather_layer(i):
        mats = layer_mats(i)
        parts = [wts[n][idx].astype(BF).reshape(2, -1) for n, idx in mats]
        shard = _pack(parts, total).reshape(2 * mrows, PACK_COLS)
        full = _all_gather_shards(f"ag_l{i}", shard).reshape(N_CHIPS, 2, total)
        out, off = {}, 0
        for (n, idx), hs in zip(mats, half_sizes(i)):
            kk, nn = wts[n].shape[1:]
            wfull = full[:, :, off:off + hs].reshape(N_CHIPS, kk, nn)
            out[n] = wfull if n in COL_SHARDED else wfull.reshape(1, N_CHIPS * kk, nn)
            off += hs
        return out

    def reduce_layer(i, grads):
        mats = layer_mats(i)
        parts = [grads[n].reshape(N_CHIPS, 2, -1) for n, _ in mats]
        packed = _pack(parts, total)
        theirs = _pair_swap(f"rs_pair_l{i}", packed)
        mine = lax.dynamic_index_in_dim(packed, ci, axis=1, keepdims=False)
        chip_sum = _add2(f"rs_add_l{i}", mine, theirs)
        landed = _chip_all_to_all(f"rs_a2a_l{i}", chip_sum)
        half = _sum_chips(f"rs_sum_l{i}", landed)
        both = _pair_gather(f"rs_gather_l{i}", half).reshape(2, total)
        out, off = {}, 0
        for (n, idx), hs in zip(mats, half_sizes(i)):
            out[n] = both[:, off:off + hs].reshape(wts[n].shape[1:])
            off += hs
        return out

    row2 = lambda a: a.reshape(1, -1)
    to3 = lambda a: a.reshape(seq, RG_HEADS, hd)
    to2 = lambda a: a.reshape(seq, dh2)

    h = x[0]
    saved = []
    for i in range(depth):
        j = i // 2
        w = gather_layer(i)
        sv = dict(w=w, h=h)
        (hn,) = _rows_fwd(f"norm_mix_l{i}", f_norm, [h], [row2(norm_mix_g[i])], [((dm,), BF)])
        sv["hn"] = hn
        if i % 2 == 0:
            u = _mm_nn(f"in_rec_l{i}", hn, w["w_in_rec"], F32)
            (v,) = _rows_fwd(f"glu_l{i}", f_glu, [Cols(u, 0, dh2), Cols(u, 1, dh2)], [], [((dh2,), F32)])
            v3 = to3(v)
            yc = to2(_conv3(f"conv_a_l{i}", v3, conv_a_full[j].reshape(kw_a, RG_HEADS, hd),
                            conv_a_b[j].reshape(RG_HEADS, hd), True))
            (ya,) = _rows_fwd(f"ln_silu_l{i}", f_ln_silu, [yc], [row2(ln_a_g[j]), row2(ln_a_b[j])], [((dh2,), BF)])
            xr3 = to3(u[:, 2 * dh2:3 * dh2])
            xc = to2(_conv3(f"conv_b_l{i}", xr3, conv_b_full[j].reshape(kw_b, RG_HEADS, hd),
                            conv_b_b[j].reshape(RG_HEADS, hd), True))
            ra = _bd_nn(f"rg_a_l{i}", xc, w_rg_a[j])
            ix = _bd_nn(f"rg_x_l{i}", xc, w_rg_x[j])
            gate_params = [row2(b_rg_a[j]), row2(b_rg_x[j]), row2(rg_lambda[j])]
            a, uu = _rows_fwd(f"gates_l{i}", f_gates, [ra, ix, xc], gate_params, [((dh2,), F32), ((dh2,), F32)])
            hs3 = _scan_fwd(f"scan_l{i}", to3(a), to3(uu))
            hs = to2(hs3)
            (yb,) = _rows_fwd(f"gelu_gate_l{i}", f_gelu_gate, [hs, Cols(u, 3, dh2)], [], [((dh2,), BF)])
            cat = jnp.concatenate([ya, yb], axis=1)
            mix = _mm_nn(f"out_rec_l{i}", cat, w["w_out_rec"], F32)
            sv.update(u=u, v3=v3, yc=yc, xr3=xr3, xc=xc, ra=ra, ix=ix, a=a, hs3=hs3, hs=hs, cat=cat,
                      gate_params=gate_params)
        else:
            qkv = _mm_nn(f"qkv_l{i}", hn, w["w_qkv"], BF)
            o, lam = _attn_fwd(f"attn_l{i}", qkv, SB_HEADS)
            mix = _mm_nn(f"o_attn_l{i}", o, w["w_o_attn"], F32)
            sv.update(qkv=qkv, o=o, lam=lam)
        h1, hn2 = _rows_fwd(f"norm_mlp_l{i}", f_add_norm, [h, mix], [row2(norm_mlp_g[i])], [((dm,), F32), ((dm,), BF)])
        up = _mm_nn(f"mlp_up_l{i}", hn2, w["w_mlp_up"], F32)
        (act,) = _rows_fwd(f"relu2_l{i}", f_relu2, [up], [], [((up.shape[1],), BF)])
        mlp = _mm_nn(f"mlp_down_l{i}", act, w["w_mlp_down"], F32)
        h2, hn3 = _rows_fwd(f"norm_ple_l{i}", f_add_norm, [h1, mlp], [row2(norm_ple_g[i])], [((dm,), F32), ((dm,), BF)])
        gpre = _mm_nn(f"ple_gate_l{i}", hn3, w["w_ple_gate"], F32)
        pb = p[i, 0].astype(BF)
        pp = _mm_nn(f"ple_proj_l{i}", pb, w["w_ple_proj"], F32)
        (h3,) = _rows_fwd(f"ple_l{i}", f_ple, [h2, pp, gpre], [], [((dm,), F32)])
        sv.update(mix=mix, h1=h1, hn2=hn2, up=up, act=act, mlp=mlp, h2=h2, hn3=hn3, gpre=gpre, pb=pb, pp=pp)
        saved.append(sv)
        h = h3

    loss_vec, dh, g_norm_f = _loss_and_grad("loss_head", h, loss_target[0], row2(norm_f_g))
    loss = lax.psum(loss_vec[0, 0], ("x", "y", "c"))

    big_grads = {n: [None] * wts[n].shape[0] for n in BIG}
    small_grads = {n: [None] * wts[n].shape[0] for n in SMALL if n != "norm_f_g"}
    for i in reversed(range(depth)):
        j = i // 2
        sv = saved[i]
        w = sv["w"]
        lg = {}
        (d_h2, d_pp, d_gpre), _ = _rows_bwd(f"ple_bwd_l{i}", f_ple, [sv["h2"], sv["pp"], sv["gpre"]], [], [dh],
                                            [F32, BF, BF])
        lg["w_ple_proj"] = _mm_tn(f"ple_proj_dw_l{i}", sv["pb"], d_pp, N_CHIPS)
        lg["w_ple_gate"] = _mm_tn(f"ple_gate_dw_l{i}", sv["hn3"], d_gpre, 1)
        d_hn3 = _mm_nt(f"ple_gate_dx_l{i}", d_gpre, w["w_ple_gate"], F32)
        (d_h1, d_mlp), (g_ple,) = _rows_bwd(f"norm_ple_bwd_l{i}", f_add_norm, [sv["h1"], sv["mlp"]],
                                            [row2(norm_ple_g[i])], [d_h2, d_hn3], [F32, BF])
        lg["w_mlp_down"] = _mm_tn(f"mlp_down_dw_l{i}", sv["act"], d_mlp, 1)
        d_act = _mm_nt(f"mlp_down_dx_l{i}", d_mlp, w["w_mlp_down"], F32)
        (d_up,), _ = _rows_bwd(f"relu2_bwd_l{i}", f_relu2, [sv["up"]], [], [d_act], [BF])
        lg["w_mlp_up"] = _mm_tn(f"mlp_up_dw_l{i}", sv["hn2"], d_up, N_CHIPS)
        d_hn2 = _mm_nt(f"mlp_up_dx_l{i}", d_up, w["w_mlp_up"], F32)
        (d_h0, d_mix), (g_mlp,) = _rows_bwd(f"norm_mlp_bwd_l{i}", f_add_norm, [sv["h"], sv["mix"]],
                                            [row2(norm_mlp_g[i])], [d_h1, d_hn2], [F32, BF])
        if i % 2 == 0:
            u = sv["u"]
            lg["w_out_rec"] = _mm_tn(f"out_rec_dw_l{i}", sv["cat"], d_mix, 1)
            d_cat = _mm_nt(f"out_rec_dx_l{i}", d_mix, w["w_out_rec"], F32)
            (d_yc,), (g_ln_g, g_ln_b) = _rows_bwd(f"ln_silu_bwd_l{i}", f_ln_silu, [sv["yc"]],
                                                  [row2(ln_a_g[j]), row2(ln_a_b[j])], [Cols(d_cat, 0, dh2)], [F32])
            d_yc3 = to3(d_yc)
            wa3 = conv_a_full[j].reshape(kw_a, RG_HEADS, hd)
            d_v = to2(_conv3(f"conv_a_dx_l{i}", d_yc3, wa3, None, False))
            g_ca_w, g_ca_b = _conv3_bwd_w(f"conv_a_dw_l{i}", d_yc3, sv["v3"], kw_a)
            (d_aval, d_agate), _ = _rows_bwd(f"glu_bwd_l{i}", f_glu, [Cols(u, 0, dh2), Cols(u, 1, dh2)], [], [d_v],
                                             [BF, BF])
            (d_hs, d_gr), _ = _rows_bwd(f"gelu_gate_bwd_l{i}", f_gelu_gate, [sv["hs"], Cols(u, 3, dh2)], [],
                                        [Cols(d_cat, 1, dh2)], [F32, BF])
            hprev3 = jnp.concatenate([jnp.zeros((1, RG_HEADS, hd), F32), sv["hs3"][:-1]], axis=0)
            da3, du3 = _scan_bwd(f"scan_bwd_l{i}", to3(sv["a"]), to3(d_hs), hprev3)
            (d_ra, d_ix, d_xc0), (g_ba, g_bx, g_lam) = _rows_bwd(
                f"gates_bwd_l{i}", f_gates, [sv["ra"], sv["ix"], sv["xc"]], sv["gate_params"],
                [to2(da3), to2(du3)], [BF, BF, F32])
            g_wa = _bd_tn(f"rg_a_dw_l{i}", sv["xc"], d_ra, RG_HEADS)
            g_wx = _bd_tn(f"rg_x_dw_l{i}", sv["xc"], d_ix, RG_HEADS)
            d_xc = _bd_nt2(f"rg_dx_l{i}", d_xc0, d_ra, w_rg_a[j], d_ix, w_rg_x[j])
            d_xc3 = to3(d_xc)
            wb3 = conv_b_full[j].reshape(kw_b, RG_HEADS, hd)
            d_xr = to2(_conv3(f"conv_b_dx_l{i}", d_xc3, wb3, None, False))
            g_cb_w, g_cb_b = _conv3_bwd_w(f"conv_b_dw_l{i}", d_xc3, sv["xr3"], kw_b)
            d_u = jnp.concatenate([d_aval, d_agate, d_xr.astype(BF), d_gr], axis=1)
            lg["w_in_rec"] = _mm_tn(f"in_rec_dw_l{i}", sv["hn"], d_u, N_CHIPS)
            d_hn = _mm_nt(f"in_rec_dx_l{i}", d_u, w["w_in_rec"], F32)
            small_grads["conv_a_w"][j] = g_ca_w.reshape(kw_a, dh2)
            small_grads["conv_a_b"][j] = g_ca_b.reshape(dh2)
            small_grads["ln_a_g"][j] = g_ln_g.reshape(dh2)
            small_grads["ln_a_b"][j] = g_ln_b.reshape(dh2)
            small_grads["conv_b_w"][j] = g_cb_w.reshape(kw_b, dh2)
            small_grads["conv_b_b"][j] = g_cb_b.reshape(dh2)
            small_grads["w_rg_a"][j] = g_wa
            small_grads["b_rg_a"][j] = g_ba.reshape(dh2)
            small_grads["w_rg_x"][j] = g_wx
            small_grads["b_rg_x"][j] = g_bx.reshape(dh2)
            small_grads["rg_lambda"][j] = g_lam.reshape(dh2)
        else:
            lg["w_o_attn"] = _mm_tn(f"o_attn_dw_l{i}", sv["o"], d_mix, 1)
            d_o = _mm_nt(f"o_attn_dx_l{i}", d_mix, w["w_o_attn"], BF)
            dq, dk, dv = _attn_bwd(f"attn_bwd_l{i}", sv["qkv"], sv["lam"], d_o, SB_HEADS)
            d_qkv = jnp.concatenate([dq, dk, dv], axis=1).astype(BF)
            lg["w_qkv"] = _mm_tn(f"qkv_dw_l{i}", sv["hn"], d_qkv, N_CHIPS)
            d_hn = _mm_nt(f"qkv_dx_l{i}", d_qkv, w["w_qkv"], F32)
        (dh,), (g_mix,) = _rows_bwd(f"norm_mix_bwd_l{i}", f_norm, [sv["h"]], [row2(norm_mix_g[i])], [d_hn], [F32],
                                    addend=d_h0)
        small_grads["norm_mix_g"][i] = g_mix.reshape(dm)
        small_grads["norm_mlp_g"][i] = g_mlp.reshape(dm)
        small_grads["norm_ple_g"][i] = g_ple.reshape(dm)
        reduced = reduce_layer(i, lg)
        for n, idx in layer_mats(i):
            big_grads[n][idx] = reduced[n]
    grad_x = dh[None]

    small_vals = {n: jnp.stack(small_grads[n]) for n in small_grads}
    small_vals["norm_f_g"] = g_norm_f.reshape(dm)
    reduced_small = _unpack_small(_small_allreduce("small_ar_grads", _pack_small(small_vals, offs, small_rows)),
                                  offs, small_shapes)
    grads = {}
    for n in SMALL:
        g = reduced_small[n]
        if n in ("conv_a_w", "conv_b_w"):
            g = lax.dynamic_slice_in_dim(g, chip * cshard, cshard, axis=2)
        grads[n] = g
    for n in BIG:
        grads[n] = jnp.stack(big_grads[n])

    delta, new_m, new_v = {}, {}, {}
    for n in WEIGHTS:
        delta[n], new_m[n], new_v[n] = _adamw(f"adamw_{n}", wts[n], grads[n], mom[n], var[n])
    return (loss, grad_x, *[grads[n] for n in WEIGHTS], *[delta[n] for n in WEIGHTS],
            *[new_m[n] for n in WEIGHTS], *[new_v[n] for n in WEIGHTS])
```

```python
import math

import jax
import jax.numpy as jnp
from jax import lax
from jax.experimental import pallas as pl
from jax.experimental.pallas import tpu as pltpu

F32 = jnp.float32
BF = jnp.bfloat16
MESH = pl.DeviceIdType.MESH

VMEM_LIMIT_BYTES = 56 * 1024 * 1024
LANES = 128
SUBLANES = 8
N_CHIPS = 4
N_DEV = 8

EPS = 1e-6
SB_HEADS = 16
RG_HEADS = 8
RG_C = 8.0
ADAM_LR = 0.001
ADAM_B1 = 0.9
ADAM_B2 = 0.999
ADAM_EPS = 1e-08
ADAM_WD = 0.01
ADAM_STEP = 10

BIG = ("w_in_rec", "w_out_rec", "w_qkv", "w_o_attn", "w_mlp_up", "w_mlp_down", "w_ple_proj", "w_ple_gate")
COL_SHARDED = ("w_in_rec", "w_qkv", "w_mlp_up", "w_ple_proj")
SMALL = ("norm_mix_g", "norm_mlp_g", "norm_ple_g", "norm_f_g", "conv_a_w", "conv_a_b", "ln_a_g", "ln_a_b",
         "conv_b_w", "conv_b_b", "w_rg_a", "b_rg_a", "w_rg_x", "b_rg_x", "rg_lambda")
WEIGHTS = ("norm_mix_g", "norm_mlp_g", "norm_ple_g", "norm_f_g", "w_in_rec", "conv_a_w", "conv_a_b", "ln_a_g",
           "ln_a_b", "conv_b_w", "conv_b_b", "w_rg_a", "b_rg_a", "w_rg_x", "b_rg_x", "rg_lambda", "w_out_rec",
           "w_qkv", "w_o_attn", "w_mlp_up", "w_mlp_down", "w_ple_proj", "w_ple_gate")


def _params(sem):
    return pltpu.CompilerParams(dimension_semantics=sem, vmem_limit_bytes=VMEM_LIMIT_BYTES)


def _pow2_floor(n):
    return 1 << (int(n).bit_length() - 1)


def _sig(x):
    return 0.5 * (jnp.tanh(0.5 * x) + 1.0)


def _softplus(x):
    return jnp.maximum(x, 0.0) + jnp.log(1.0 + jnp.exp(-jnp.maximum(x, -x)))


def _rms(x, g):
    return x * lax.rsqrt(jnp.mean(x * x, axis=-1, keepdims=True) + EPS) * g


def _neg_expm1(x):
    series = -x * (1.0 + 0.5 * x * (1.0 + x * (1.0 / 3.0) * (1.0 + 0.25 * x)))
    return jnp.where(x > -1e-2, series, 1.0 - jnp.exp(x))


def f_norm(h, g):
    return (_rms(h, g),)


def f_add_norm(h, mix, g):
    h1 = h + mix
    return h1, _rms(h1, g)


def f_ple(h, pp, gpre):
    return (h + pp * _sig(gpre),)


def f_relu2(u):
    r = jnp.maximum(u, 0.0)
    return (r * r,)


def f_glu(a, b):
    return (a * _sig(b),)


def f_ln_silu(x, g, b):
    mu = jnp.mean(x, axis=-1, keepdims=True)
    xc = x - mu
    var = jnp.mean(xc * xc, axis=-1, keepdims=True)
    y = xc * lax.rsqrt(var + EPS) * g + b
    return (y * _sig(y),)


def f_gelu_gate(hs, gr):
    inner = math.sqrt(2.0 / math.pi) * (gr + 0.044715 * gr * gr * gr)
    return (hs * (0.5 * gr * (1.0 + jnp.tanh(inner))),)


def f_gates(ra, ix, xc, b_a, b_x, lam):
    r = _sig(ra + b_a)
    i = _sig(ix + b_x)
    log_a = -RG_C * r * _softplus(-lam)
    a = jnp.exp(log_a)
    mult = jnp.sqrt(_neg_expm1(2.0 * log_a))
    return a, mult * (i * xc)


class Cols:
    def __init__(self, arr, blk, width):
        self.arr, self.blk, self.width = arr, blk, width
        self.shape = (arr.shape[0], width)
        self.dtype = arr.dtype


def _row_spec(a, t):
    if isinstance(a, Cols):
        blk = a.blk
        return pl.BlockSpec((t, a.width), lambda i: (i, blk))
    nd = len(a.shape)
    return pl.BlockSpec((t,) + tuple(a.shape[1:]), lambda i: (i,) + (0,) * (nd - 1))


def _full_spec(a):
    nd = len(a.shape)
    return pl.BlockSpec(tuple(a.shape), lambda i: (0,) * nd)


def _arr(a):
    return a.arr if isinstance(a, Cols) else a


def _row_tile(shapes):
    s = shapes[0][0]
    widest = max(int(math.prod(sh[1:])) for sh in shapes)
    t = _pow2_floor(max(16, (1 << 18) // widest))
    t = min(t, s)
    assert s % t == 0
    return t


def _rows_fwd(name, fn, rows, params, outs):
    s = rows[0].shape[0]
    t = _row_tile([r.shape for r in rows] + [(s,) + tuple(o[0]) for o in outs])
    nin = len(rows) + len(params)

    def body(*refs):
        vals = [r[...].astype(F32) for r in refs[:nin]]
        res = fn(*vals)
        for o_ref, v in zip(refs[nin:], res):
            o_ref[...] = v.astype(o_ref.dtype)

    out_shape = [jax.ShapeDtypeStruct((s,) + tuple(o[0]), o[1]) for o in outs]
    res = pl.pallas_call(
        body, grid=(s // t,), name=name,
        in_specs=[_row_spec(r, t) for r in rows] + [_full_spec(p) for p in params],
        out_specs=[_row_spec(o, t) for o in out_shape], out_shape=out_shape,
        compiler_params=_params(("parallel",)),
    )(*[_arr(r) for r in rows], *params)
    return res


def _rows_bwd(name, fn, rows, params, cots, out_dtypes, addend=None):
    s = rows[0].shape[0]
    nr, npar, nc = len(rows), len(params), len(cots)
    extra = [addend] if addend is not None else []
    t = _row_tile([r.shape for r in rows] + [c.shape for c in cots])

    def body(*refs):
        rs = [r[...].astype(F32) for r in refs[:nr]]
        ps = [r[...].astype(F32) for r in refs[nr:nr + npar]]
        cs = tuple(r[...].astype(F32) for r in refs[nr + npar:nr + npar + nc])
        k = nr + npar + nc
        ad = refs[k][...].astype(F32) if extra else None
        k += len(extra)
        grow = refs[k:k + nr]
        gpar = refs[k + nr:]
        _, vjp = jax.vjp(fn, *rs, *ps)
        g = vjp(cs)
        for j in range(nr):
            val = g[j]
            if j == 0 and ad is not None:
                val = val + ad
            grow[j][...] = val.astype(grow[j].dtype)
        first = pl.program_id(0) == 0
        for j in range(npar):
            @pl.when(first)
            def _(j=j):
                gpar[j][...] = jnp.zeros_like(gpar[j])
            gpar[j][...] += g[nr + j]

    out_shape = ([jax.ShapeDtypeStruct(tuple(r.shape), dt) for r, dt in zip(rows, out_dtypes)]
                 + [jax.ShapeDtypeStruct(tuple(p.shape), F32) for p in params])
    res = pl.pallas_call(
        body, grid=(s // t,), name=name,
        in_specs=([_row_spec(r, t) for r in rows] + [_full_spec(p) for p in params]
                  + [_row_spec(c, t) for c in cots] + [_row_spec(a, t) for a in extra]),
        out_specs=([_row_spec(o, t) for o in out_shape[:nr]] + [_full_spec(o) for o in out_shape[nr:]]),
        out_shape=out_shape,
        compiler_params=_params(("arbitrary",)),
    )(*[_arr(r) for r in rows], *params, *[_arr(c) for c in cots], *extra)
    return list(res[:nr]), list(res[nr:])


def _loss_and_grad(name, h, tgt, g):
    s, d = h.shape
    t = _row_tile([h.shape])

    def body(h_ref, t_ref, g_ref, loss_ref, dh_ref, dg_ref):
        tg = t_ref[...]

        def f(hh, gg):
            e = _rms(hh, gg) - tg
            return 0.5 * jnp.mean(e * e, axis=-1, keepdims=True)

        val, vjp = jax.vjp(f, h_ref[...], g_ref[...])
        dh, dg = vjp(jnp.ones_like(val))
        dh_ref[...] = dh

        @pl.when(pl.program_id(0) == 0)
        def _():
            loss_ref[...] = jnp.zeros_like(loss_ref)
            dg_ref[...] = jnp.zeros_like(dg_ref)
        loss_ref[...] += jnp.broadcast_to(jnp.sum(val, axis=0, keepdims=True), loss_ref.shape)
        dg_ref[...] += dg

    return pl.pallas_call(
        body, grid=(s // t,), name=name,
        in_specs=[_row_spec(h, t), _row_spec(tgt, t), _full_spec(g)],
        out_specs=[pl.BlockSpec((1, LANES), lambda i: (0, 0)), _row_spec(h, t), _full_spec(g)],
        out_shape=[jax.ShapeDtypeStruct((1, LANES), F32), jax.ShapeDtypeStruct((s, d), F32),
                   jax.ShapeDtypeStruct(tuple(g.shape), F32)],
        compiler_params=_params(("arbitrary",)),
    )(h, tgt, g)


def _col_tile(nc):
    for t in (1024, 768, 512):
        if nc >= t and nc % t == 0:
            return t
    return nc


def _mm(name, a, b, grid, in_specs, out_spec, out_shape, dims, acc_shape):
    nsteps = grid[2]

    def body(a_ref, b_ref, o_ref, acc_ref):
        k = pl.program_id(2)

        @pl.when(k == 0)
        def _():
            acc_ref[...] = jnp.zeros_like(acc_ref)
        acc_ref[...] += lax.dot_general(a_ref[...].astype(BF), b_ref[...].astype(BF), dims,
                                        preferred_element_type=F32)

        @pl.when(k == nsteps - 1)
        def _():
            o_ref[...] = acc_ref[...].astype(o_ref.dtype)

    return pl.pallas_call(
        body, grid=grid, name=name, in_specs=in_specs, out_specs=out_spec, out_shape=out_shape,
        scratch_shapes=[pltpu.VMEM(acc_shape, F32)],
        compiler_params=_params(("parallel", "parallel", "arbitrary")),
    )(a, b)


def _mm_nn(name, a, w, out_dtype):
    m, k = a.shape
    g, k2, nc = w.shape
    assert k == k2
    tm, tk, tn = min(m, 1024), min(k, 1024), _col_tile(nc)
    r = nc // tn
    return _mm(name, a, w, (m // tm, g * r, k // tk),
               [pl.BlockSpec((tm, tk), lambda i, j, l: (i, l)),
                pl.BlockSpec((None, tk, tn), lambda i, j, l: (j // r, l, j % r))],
               pl.BlockSpec((tm, tn), lambda i, j, l: (i, j)),
               jax.ShapeDtypeStruct((m, g * nc), out_dtype), (((1,), (0,)), ((), ())), (tm, tn))


def _mm_nt(name, a, w, out_dtype):
    m, n = a.shape
    g, k, nc = w.shape
    assert n == g * nc
    tm, tk, tn = min(m, 1024), min(k, 1024), _col_tile(nc)
    r = nc // tn
    return _mm(name, a, w, (m // tm, k // tk, g * r),
               [pl.BlockSpec((tm, tn), lambda i, j, l: (i, l)),
                pl.BlockSpec((None, tk, tn), lambda i, j, l: (l // r, j, l % r))],
               pl.BlockSpec((tm, tk), lambda i, j, l: (i, j)),
               jax.ShapeDtypeStruct((m, k), out_dtype), (((1,), (1,)), ((), ())), (tm, tk))


def _mm_tn(name, a, b, g):
    m, k = a.shape
    m2, n = b.shape
    assert m == m2 and n % g == 0
    nc = n // g
    tm, tk, tn = min(m, 1024), min(k, 1024), _col_tile(nc)
    r = nc // tn
    return _mm(name, a, b, (k // tk, g * r, m // tm),
               [pl.BlockSpec((tm, tk), lambda i, j, l: (l, i)),
                pl.BlockSpec((tm, tn), lambda i, j, l: (l, j))],
               pl.BlockSpec((None, tk, tn), lambda i, j, l: (j // r, i, j % r)),
               jax.ShapeDtypeStruct((g, k, nc), F32), (((0,), (0,)), ((), ())), (tk, tn))


def _bd_nn(name, x, w):
    s, c = x.shape
    nh, hd, _ = w.shape
    tm = min(s, 1024)

    def body(x_ref, w_ref, o_ref):
        o_ref[...] = jnp.dot(x_ref[...].astype(BF), w_ref[...].astype(BF), preferred_element_type=F32)

    return pl.pallas_call(
        body, grid=(s // tm, nh), name=name,
        in_specs=[pl.BlockSpec((tm, hd), lambda i, h: (i, h)), pl.BlockSpec((None, hd, hd), lambda i, h: (h, 0, 0))],
        out_specs=pl.BlockSpec((tm, hd), lambda i, h: (i, h)),
        out_shape=jax.ShapeDtypeStruct((s, c), F32),
        compiler_params=_params(("parallel", "parallel")),
    )(x, w)


def _bd_nt2(name, add, dy1, w1, dy2, w2):
    s, c = add.shape
    nh, hd, _ = w1.shape
    tm = min(s, 1024)
    nt = (((1,), (1,)), ((), ()))

    def body(a_ref, d1_ref, w1_ref, d2_ref, w2_ref, o_ref):
        o_ref[...] = (a_ref[...]
                      + lax.dot_general(d1_ref[...].astype(BF), w1_ref[...].astype(BF), nt, preferred_element_type=F32)
                      + lax.dot_general(d2_ref[...].astype(BF), w2_ref[...].astype(BF), nt, preferred_element_type=F32))

    row = pl.BlockSpec((tm, hd), lambda i, h: (i, h))
    wsp = pl.BlockSpec((None, hd, hd), lambda i, h: (h, 0, 0))
    return pl.pallas_call(
        body, grid=(s // tm, nh), name=name, in_specs=[row, row, wsp, row, wsp], out_specs=row,
        out_shape=jax.ShapeDtypeStruct((s, c), F32), compiler_params=_params(("parallel", "parallel")),
    )(add, dy1, w1, dy2, w2)


def _bd_tn(name, x, dy, nh):
    s, c = x.shape
    hd = c // nh
    tm = min(s, 1024)
    tn = (((0,), (0,)), ((), ()))

    def body(x_ref, d_ref, o_ref):
        @pl.when(pl.program_id(1) == 0)
        def _():
            o_ref[...] = jnp.zeros_like(o_ref)
        o_ref[...] += lax.dot_general(x_ref[...].astype(BF), d_ref[...].astype(BF), tn, preferred_element_type=F32)

    row = pl.BlockSpec((tm, hd), lambda h, i: (i, h))
    return pl.pallas_call(
        body, grid=(nh, s // tm), name=name, in_specs=[row, row],
        out_specs=pl.BlockSpec((None, hd, hd), lambda h, i: (h, 0, 0)),
        out_shape=jax.ShapeDtypeStruct((nh, hd, hd), F32), compiler_params=_params(("parallel", "arbitrary")),
    )(x, dy)


CONV_HALO = 32
CONV_SUB = 16


def _conv3(name, x3, w3, b2, causal):
    s, sl, hd = x3.shape
    kw = w3.shape[0]
    tc = min(s, 256)
    nchunks = s // tc
    per = tc // CONV_HALO
    nhalo = s // CONV_HALO
    assert kw - 1 <= CONV_HALO and tc % CONV_HALO == 0 and tc % CONV_SUB == 0
    has_bias = b2 is not None

    def body(*refs):
        if has_bias:
            cur_ref, halo_ref, w_ref, b_ref, y_ref, win_ref = refs
        else:
            cur_ref, halo_ref, w_ref, y_ref, win_ref = refs
        i = pl.program_id(0)
        if causal:
            win_ref[0:CONV_HALO] = jnp.where(i > 0, halo_ref[...], 0.0)
            win_ref[CONV_HALO:CONV_HALO + tc] = cur_ref[...]
        else:
            win_ref[0:tc] = cur_ref[...]
            win_ref[tc:tc + CONV_HALO] = jnp.where(i < nchunks - 1, halo_ref[...], 0.0)

        def sub_step(j, carry):
            t0 = pl.multiple_of(j * CONV_SUB, CONV_SUB)
            if has_bias:
                acc = jnp.broadcast_to(b_ref[...], (CONV_SUB, sl, hd))
            else:
                acc = jnp.zeros((CONV_SUB, sl, hd), F32)
            for k in range(kw):
                off = CONV_HALO - (kw - 1) + k if causal else kw - 1 - k
                acc = acc + w_ref[k] * win_ref[pl.ds(t0 + off, CONV_SUB)]
            y_ref[pl.ds(t0, CONV_SUB)] = acc
            return carry

        lax.fori_loop(0, tc // CONV_SUB, sub_step, 0)

    if causal:
        halo_map = lambda i: (jnp.maximum(i * per - 1, 0), 0, 0)
    else:
        halo_map = lambda i: (jnp.minimum((i + 1) * per, nhalo - 1), 0, 0)
    chunk = pl.BlockSpec((tc, sl, hd), lambda i: (i, 0, 0))
    in_specs = [chunk, pl.BlockSpec((CONV_HALO, sl, hd), halo_map), pl.BlockSpec((kw, sl, hd), lambda i: (0, 0, 0))]
    args = [x3, x3, w3]
    if has_bias:
        in_specs.append(pl.BlockSpec((sl, hd), lambda i: (0, 0)))
        args.append(b2)
    return pl.pallas_call(
        body, grid=(nchunks,), name=name, in_specs=in_specs, out_specs=chunk,
        out_shape=jax.ShapeDtypeStruct((s, sl, hd), F32),
        scratch_shapes=[pltpu.VMEM((tc + CONV_HALO, sl, hd), F32)],
        compiler_params=_params(("parallel",)),
    )(*args)


def _conv3_bwd_w(name, dy3, x3, kw):
    s, sl, hd = x3.shape
    tc = min(s, 256)
    per = tc // CONV_HALO

    def body(dy_ref, cur_ref, halo_ref, dw_ref, db_ref, win_ref):
        i = pl.program_id(0)

        @pl.when(i == 0)
        def _():
            dw_ref[...] = jnp.zeros_like(dw_ref)
            db_ref[...] = jnp.zeros_like(db_ref)
        win_ref[0:CONV_HALO] = jnp.where(i > 0, halo_ref[...], 0.0)
        win_ref[CONV_HALO:CONV_HALO + tc] = cur_ref[...]

        def sub_step(j, carry):
            t0 = pl.multiple_of(j * SUBLANES, SUBLANES)
            dy = dy_ref[pl.ds(t0, SUBLANES)]
            new = [carry[k] + jnp.sum(dy * win_ref[pl.ds(t0 + CONV_HALO - (kw - 1) + k, SUBLANES)], axis=0)
                   for k in range(kw)]
            new.append(carry[kw] + jnp.sum(dy, axis=0))
            return tuple(new)

        zero = jnp.zeros((sl, hd), F32)
        res = lax.fori_loop(0, tc // SUBLANES, sub_step, tuple(zero for _ in range(kw + 1)))
        for k in range(kw):
            dw_ref[k] += res[k]
        db_ref[...] += res[kw]

    chunk = pl.BlockSpec((tc, sl, hd), lambda i: (i, 0, 0))
    return pl.pallas_call(
        body, grid=(s // tc,), name=name,
        in_specs=[chunk, chunk, pl.BlockSpec((CONV_HALO, sl, hd), lambda i: (jnp.maximum(i * per - 1, 0), 0, 0))],
        out_specs=[pl.BlockSpec((kw, sl, hd), lambda i: (0, 0, 0)), pl.BlockSpec((sl, hd), lambda i: (0, 0))],
        out_shape=[jax.ShapeDtypeStruct((kw, sl, hd), F32), jax.ShapeDtypeStruct((sl, hd), F32)],
        scratch_shapes=[pltpu.VMEM((tc + CONV_HALO, sl, hd), F32)],
        compiler_params=_params(("arbitrary",)),
    )(dy3, x3, x3)


def _scan_fwd(name, a3, u3):
    s, sl, hd = a3.shape
    tc = min(s, 512)

    def body(a_ref, u_ref, h_ref, carry_ref):
        @pl.when(pl.program_id(0) == 0)
        def _():
            carry_ref[...] = jnp.zeros_like(carry_ref)

        def step(t, h):
            h = a_ref[t] * h + u_ref[t]
            h_ref[t] = h
            return h

        carry_ref[...] = lax.fori_loop(0, tc, step, carry_ref[...], unroll=8)

    chunk = pl.BlockSpec((tc, sl, hd), lambda i: (i, 0, 0))
    return pl.pallas_call(
        body, grid=(s // tc,), name=name, in_specs=[chunk, chunk], out_specs=chunk,
        out_shape=jax.ShapeDtypeStruct((s, sl, hd), F32), scratch_shapes=[pltpu.VMEM((sl, hd), F32)],
        compiler_params=_params(("arbitrary",)),
    )(a3, u3)


def _scan_bwd(name, a3, gh3, hprev3):
    s, sl, hd = a3.shape
    tc = min(s, 512)
    n = s // tc

    def body(a_ref, g_ref, hp_ref, da_ref, du_ref, carry_ref):
        @pl.when(pl.program_id(0) == 0)
        def _():
            carry_ref[...] = jnp.zeros_like(carry_ref)

        def step(j, c):
            t = tc - 1 - j
            lam = g_ref[t] + c
            du_ref[t] = lam
            da_ref[t] = lam * hp_ref[t]
            return a_ref[t] * lam

        carry_ref[...] = lax.fori_loop(0, tc, step, carry_ref[...], unroll=8)

    chunk = pl.BlockSpec((tc, sl, hd), lambda i: (n - 1 - i, 0, 0))
    shp = jax.ShapeDtypeStruct((s, sl, hd), F32)
    return pl.pallas_call(
        body, grid=(n,), name=name, in_specs=[chunk, chunk, chunk], out_specs=[chunk, chunk],
        out_shape=[shp, shp], scratch_shapes=[pltpu.VMEM((sl, hd), F32)],
        compiler_params=_params(("arbitrary",)),
    )(a3, gh3, hprev3)


ATT_Q = 512
ATT_K = 128
NT_DIMS = (((1,), (1,)), ((), ()))
TN_DIMS = (((0,), (0,)), ((), ()))


def _attn_tiles(s):
    tq = min(s, ATT_Q)
    kb = min(tq, ATT_K)
    assert s % tq == 0 and tq % kb == 0
    return tq, kb


def _hilo_dot(x, u):
    hi = x.astype(BF)
    lo = (x - hi.astype(F32)).astype(BF)
    return jnp.dot(hi, u, preferred_element_type=F32) + jnp.dot(lo, u, preferred_element_type=F32)


def _attn_fwd(name, qkv, nh):
    s, d3 = qkv.shape
    d = d3 // 3
    dh = d // nh
    tq, kb = _attn_tiles(s)
    per = tq // kb
    scale = 1.0 / math.sqrt(dh)

    def body(q_ref, k_ref, v_ref, o_ref, lam_ref):
        i = pl.program_id(1)
        q = q_ref[...]
        after = (lax.broadcasted_iota(jnp.int32, (kb, kb), 0)
                 > lax.broadcasted_iota(jnp.int32, (kb, kb), 1)).astype(BF)
        qpos = i * tq + lax.broadcasted_iota(jnp.int32, (tq, kb), 0)
        kcol = lax.broadcasted_iota(jnp.int32, (tq, kb), 1)

        def block(b, carry, masked):
            acc, tail_carry = carry
            off = pl.multiple_of(b * kb, kb)
            kt = k_ref[pl.ds(off, kb), :]
            vt = v_ref[pl.ds(off, kb), :]
            z = lax.dot_general(q, kt, NT_DIMS, preferred_element_type=F32) * scale
            lraw = -_softplus(z)
            if masked:
                mask = (off + kcol) < qpos
                lm = jnp.where(mask, lraw, 0.0)
            else:
                lm = lraw
            w = jnp.exp(z + lraw + _hilo_dot(lm, after) + tail_carry)
            if masked:
                w = jnp.where(mask, w, 0.0)
            acc = acc + jnp.dot(w.astype(BF), vt, preferred_element_type=F32)
            return acc, tail_carry + jnp.sum(lm, axis=1, keepdims=True)

        carry = (jnp.zeros((tq, dh), F32), jnp.zeros((tq, 1), F32))
        carry = lax.fori_loop(0, per, lambda jj, c: block((i + 1) * per - 1 - jj, c, True), carry)
        acc, total = lax.fori_loop(0, i * per, lambda jj, c: block(i * per - 1 - jj, c, False), carry)
        o_ref[...] = acc.astype(o_ref.dtype)
        lam_ref[...] = total

    return pl.pallas_call(
        body, grid=(nh, s // tq), name=name,
        in_specs=[pl.BlockSpec((tq, dh), lambda h, i: (i, h)),
                  pl.BlockSpec((s, dh), lambda h, i: (0, nh + h)),
                  pl.BlockSpec((s, dh), lambda h, i: (0, 2 * nh + h))],
        out_specs=[pl.BlockSpec((tq, dh), lambda h, i: (i, h)), pl.BlockSpec((None, tq, 1), lambda h, i: (h, i, 0))],
        out_shape=[jax.ShapeDtypeStruct((s, d), BF), jax.ShapeDtypeStruct((nh, s, 1), F32)],
        compiler_params=_params(("parallel", "arbitrary")),
    )(qkv, qkv, qkv)


def _attn_bwd(name, qkv, lam, do, nh):
    s, d3 = qkv.shape
    d = d3 // 3
    dh = d // nh
    tq, kb = _attn_tiles(s)
    per = tq // kb
    scale = 1.0 / math.sqrt(dh)

    def body(q_ref, k_ref, v_ref, do_ref, lam_ref, dq_ref, dk_ref, dv_ref):
        i = pl.program_id(1)

        @pl.when(i == 0)
        def _():
            dk_ref[...] = jnp.zeros_like(dk_ref)
            dv_ref[...] = jnp.zeros_like(dv_ref)
        q = q_ref[...]
        dout = do_ref[...]
        total = lam_ref[...]
        row = lax.broadcasted_iota(jnp.int32, (kb, kb), 0)
        col = lax.broadcasted_iota(jnp.int32, (kb, kb), 1)
        upto = (row <= col).astype(BF)
        before = (row < col).astype(BF)
        qpos = i * tq + lax.broadcasted_iota(jnp.int32, (tq, kb), 0)
        kcol = lax.broadcasted_iota(jnp.int32, (tq, kb), 1)

        def block(b, carry, masked):
            dq, l_carry, g_carry = carry
            off = pl.multiple_of(b * kb, kb)
            kt = k_ref[pl.ds(off, kb), :]
            vt = v_ref[pl.ds(off, kb), :]
            z = lax.dot_general(q, kt, NT_DIMS, preferred_element_type=F32) * scale
            lraw = -_softplus(z)
            if masked:
                mask = (off + kcol) < qpos
                lm = jnp.where(mask, lraw, 0.0)
            else:
                lm = lraw
            logsig = z + lraw
            w = jnp.exp(logsig + total - (_hilo_dot(lm, upto) + l_carry))
            if masked:
                w = jnp.where(mask, w, 0.0)
            g = w * lax.dot_general(dout, vt, NT_DIMS, preferred_element_type=F32)
            sg = jnp.exp(logsig)
            dz = g * (1.0 - sg) - sg * (_hilo_dot(g, before) + g_carry)
            if masked:
                dz = jnp.where(mask, dz, 0.0)
            dz = (dz * scale).astype(BF)
            dq = dq + jnp.dot(dz, kt, preferred_element_type=F32)
            dk_ref[pl.ds(off, kb), :] += lax.dot_general(dz, q, TN_DIMS, preferred_element_type=F32)
            dv_ref[pl.ds(off, kb), :] += lax.dot_general(w.astype(BF), dout, TN_DIMS, preferred_element_type=F32)
            return (dq, l_carry + jnp.sum(lm, axis=1, keepdims=True), g_carry + jnp.sum(g, axis=1, keepdims=True))

        zero_col = jnp.zeros((tq, 1), F32)
        carry = (jnp.zeros((tq, dh), F32), zero_col, zero_col)
        carry = lax.fori_loop(0, i * per, lambda b, c: block(b, c, False), carry)
        dq, _, _ = lax.fori_loop(0, per, lambda jj, c: block(i * per + jj, c, True), carry)
        dq_ref[...] = dq

    blk = pl.BlockSpec((tq, dh), lambda h, i: (i, h))
    head = pl.BlockSpec((s, dh), lambda h, i: (0, h))
    shp = jax.ShapeDtypeStruct((s, d), F32)
    return pl.pallas_call(
        body, grid=(nh, s // tq), name=name,
        in_specs=[blk, pl.BlockSpec((s, dh), lambda h, i: (0, nh + h)),
                  pl.BlockSpec((s, dh), lambda h, i: (0, 2 * nh + h)), blk,
                  pl.BlockSpec((None, tq, 1), lambda h, i: (h, i, 0))],
        out_specs=[blk, head, head], out_shape=[shp, shp, shp],
        compiler_params=_params(("parallel", "arbitrary")),
    )(qkv, qkv, qkv, do, lam)


def _adamw(name, w, g, m, v):
    shape = w.shape
    c = shape[-1]
    r = int(math.prod(shape[:-1])) if len(shape) > 1 else 1
    w2, g2, m2, v2 = (a.reshape(r, c) for a in (w, g, m, v))
    t = min(r, max(SUBLANES, _pow2_floor((1 << 19) // (4 * c))))
    if r % t:
        t = r
    assert r * c * 4 <= (1 << 22) or t < r

    def body(w_ref, g_ref, m_ref, v_ref, d_ref, nm_ref, nv_ref):
        gg = g_ref[...]
        nm = ADAM_B1 * m_ref[...] + (1.0 - ADAM_B1) * gg
        nv = ADAM_B2 * v_ref[...] + (1.0 - ADAM_B2) * (gg * gg)
        m_hat = nm / (1.0 - ADAM_B1 ** ADAM_STEP)
        v_hat = nv / (1.0 - ADAM_B2 ** ADAM_STEP)
        d_ref[...] = -ADAM_LR * (m_hat / (jnp.sqrt(v_hat) + ADAM_EPS) + ADAM_WD * w_ref[...])
        nm_ref[...] = nm
        nv_ref[...] = nv

    spec = pl.BlockSpec((t, c), lambda i: (i, 0))
    shp = jax.ShapeDtypeStruct((r, c), F32)
    d, nm, nv = pl.pallas_call(
        body, grid=(r // t,), name=name, in_specs=[spec] * 4, out_specs=[spec] * 3, out_shape=[shp] * 3,
        compiler_params=_params(("parallel",)),
    )(w2, g2, m2, v2)
    return d.reshape(shape), nm.reshape(shape), nv.reshape(shape)


def _add_row_tile(m, c):
    t = min(m, max(SUBLANES, _pow2_floor((1 << 18) // c)))
    assert m % t == 0
    return t


def _add_half(name, g, theirs, core):
    nb, _, m, c = g.shape
    t = _add_row_tile(m, c)

    def body(core_ref, a_ref, b_ref, o_ref):
        o_ref[...] = a_ref[...] + b_ref[...]

    spec = pl.BlockSpec((None, t, c), lambda k, i, core_ref: (k, i, 0))
    return pl.pallas_call(
        body, name=name, out_shape=jax.ShapeDtypeStruct((nb, m, c), F32),
        grid_spec=pltpu.PrefetchScalarGridSpec(
            num_scalar_prefetch=1, grid=(nb, m // t),
            in_specs=[pl.BlockSpec((None, None, t, c), lambda k, i, core_ref: (k, core_ref[0], i, 0)), spec],
            out_specs=spec),
        compiler_params=_params(("parallel", "parallel")))(core, g, theirs)


def _sum_chips(name, a):
    nb, m, c = a.shape
    t = _add_row_tile(m, c)

    def body(a_ref, o_ref):
        acc = a_ref[0]
        for q in range(1, nb):
            acc = acc + a_ref[q]
        o_ref[...] = acc

    return pl.pallas_call(body, grid=(m // t,), name=name,
                          in_specs=[pl.BlockSpec((nb, t, c), lambda i: (0, i, 0))],
                          out_specs=pl.BlockSpec((t, c), lambda i: (i, 0)),
                          out_shape=jax.ShapeDtypeStruct((m, c), F32),
                          compiler_params=_params(("parallel",)))(a)


HBM_SPEC = pl.BlockSpec(memory_space=pltpu.HBM)


def _me():
    return lax.axis_index("x"), lax.axis_index("y"), lax.axis_index("c")


def _remote(src, dst, send_sem, recv_sem, dev):
    return pltpu.make_async_remote_copy(src_ref=src, dst_ref=dst, send_sem=send_sem, recv_sem=recv_sem,
                                        device_id=dev, device_id_type=MESH)


def _other_chips(x, y):
    return [(px, py, 2 * px + py) for px, py in ((1 - x, y), (x, 1 - y), (1 - x, 1 - y))]


def _all_gather_shards(name, shards):
    nw = len(shards)
    halves = [sh.shape[0] // 2 for sh in shards]

    def body(*refs):
        x_refs, out_refs = refs[:nw], refs[nw:2 * nw]
        send_sems, recv_sems, local_sems = refs[2 * nw:]
        x, y, c = _me()
        me, sibling = (x, y, c), (x, y, 1 - c)
        chips = _other_chips(x, y)
        mychip = 2 * x + y

        def blk(i, q, pc):
            return out_refs[i].at[q, pl.ds(pc * halves[i], halves[i]), :]

        def half(i):
            return x_refs[i].at[pl.ds(c * halves[i], halves[i]), :]

        local = [pltpu.make_async_copy(x_refs[i], out_refs[i].at[mychip], local_sems.at[i]) for i in range(nw)]
        for cp in local:
            cp.start()
        first = [_remote(half(i), blk(i, mychip, c), send_sems.at[6 * i + k], recv_sems.at[6 * i + k], (px, py, c))
                 for i in range(nw) for k, (px, py, q) in enumerate(chips)]
        for cp in first:
            cp.start()
        passed = []
        for i in range(nw):
            for k, (px, py, q) in enumerate(chips):
                _remote(half(i), blk(i, q, c), send_sems.at[6 * i + k], recv_sems.at[6 * i + k], me).wait_recv()
                fwd = _remote(blk(i, q, c), blk(i, q, c), send_sems.at[6 * i + 3 + k], recv_sems.at[6 * i + 3 + k],
                              sibling)
                fwd.start()
                passed.append(fwd)
        for i in range(nw):
            for k, (px, py, q) in enumerate(chips):
                _remote(half(i), blk(i, q, 1 - c), send_sems.at[6 * i + 3 + k], recv_sems.at[6 * i + 3 + k],
                        me).wait_recv()
        for cp in first + passed:
            cp.wait_send()
        for cp in local:
            cp.wait()

    return pl.pallas_call(
        body, name=name,
        out_shape=[jax.ShapeDtypeStruct((N_CHIPS,) + tuple(sh.shape), sh.dtype) for sh in shards],
        in_specs=[HBM_SPEC] * nw, out_specs=[HBM_SPEC] * nw,
        scratch_shapes=[pltpu.SemaphoreType.DMA((6 * nw,)), pltpu.SemaphoreType.DMA((6 * nw,)),
                        pltpu.SemaphoreType.DMA((nw,))],
    )(*shards)


def _pair_swap(name, srcs):
    nw = len(srcs)

    def body(*refs):
        src_refs, out_refs = refs[:nw], refs[nw:2 * nw]
        send_sems, recv_sems = refs[2 * nw:]
        x, y, c = _me()
        cps = [_remote(src_refs[i].at[k, 1 - c], out_refs[i].at[k], send_sems.at[N_CHIPS * i + k],
                       recv_sems.at[N_CHIPS * i + k], (x, y, 1 - c))
               for i in range(nw) for k in range(N_CHIPS)]
        for cp in cps:
            cp.start()
        for cp in cps:
            cp.wait()

    return pl.pallas_call(
        body, name=name,
        out_shape=[jax.ShapeDtypeStruct((N_CHIPS,) + tuple(s.shape[2:]), s.dtype) for s in srcs],
        in_specs=[HBM_SPEC] * nw, out_specs=[HBM_SPEC] * nw,
        scratch_shapes=[pltpu.SemaphoreType.DMA((N_CHIPS * nw,)), pltpu.SemaphoreType.DMA((N_CHIPS * nw,))],
    )(*srcs)


def _chip_all_to_all(name, ps):
    nw = len(ps)

    def body(*refs):
        p_refs, out_refs = refs[:nw], refs[nw:2 * nw]
        send_sems, recv_sems, local_sems = refs[2 * nw:]
        x, y, c = _me()
        chips = _other_chips(x, y)
        mychip = 2 * x + y
        local = [pltpu.make_async_copy(p_refs[i].at[mychip], out_refs[i].at[mychip], local_sems.at[i])
                 for i in range(nw)]
        for cp in local:
            cp.start()
        cps = [_remote(p_refs[i].at[q], out_refs[i].at[mychip], send_sems.at[3 * i + k], recv_sems.at[3 * i + k],
                       (px, py, c))
               for i in range(nw) for k, (px, py, q) in enumerate(chips)]
        for cp in cps:
            cp.start()
        for i in range(nw):
            for k, (px, py, q) in enumerate(chips):
                _remote(p_refs[i].at[q], out_refs[i].at[q], send_sems.at[3 * i + k], recv_sems.at[3 * i + k],
                        (px, py, c)).wait_recv()
        for cp in cps:
            cp.wait_send()
        for cp in local:
            cp.wait()

    return pl.pallas_call(
        body, name=name, out_shape=[jax.ShapeDtypeStruct(tuple(p.shape), p.dtype) for p in ps],
        in_specs=[HBM_SPEC] * nw, out_specs=[HBM_SPEC] * nw,
        scratch_shapes=[pltpu.SemaphoreType.DMA((3 * nw,)), pltpu.SemaphoreType.DMA((3 * nw,)),
                        pltpu.SemaphoreType.DMA((nw,))],
    )(*ps)


def _pair_gather(name, hs):
    nw = len(hs)

    def body(*refs):
        h_refs, out_refs = refs[:nw], refs[nw:2 * nw]
        send_sems, recv_sems, local_sems = refs[2 * nw:]
        x, y, c = _me()
        sibling = (x, y, 1 - c)
        local = [pltpu.make_async_copy(h_refs[i], out_refs[i].at[c], local_sems.at[i]) for i in range(nw)]
        for cp in local:
            cp.start()
        cps = [_remote(h_refs[i], out_refs[i].at[c], send_sems.at[i], recv_sems.at[i], sibling) for i in range(nw)]
        for cp in cps:
            cp.start()
        for i in range(nw):
            _remote(h_refs[i], out_refs[i].at[1 - c], send_sems.at[i], recv_sems.at[i], sibling).wait_recv()
        for cp in cps:
            cp.wait_send()
        for cp in local:
            cp.wait()

    return pl.pallas_call(
        body, name=name, out_shape=[jax.ShapeDtypeStruct((2,) + tuple(h.shape), h.dtype) for h in hs],
        in_specs=[HBM_SPEC] * nw, out_specs=[HBM_SPEC] * nw,
        scratch_shapes=[pltpu.SemaphoreType.DMA((nw,)), pltpu.SemaphoreType.DMA((nw,)),
                        pltpu.SemaphoreType.DMA((nw,))],
    )(*hs)


def _small_allreduce(name, buf):
    r, n = buf.shape

    def body(x_ref, o_ref, land_ref, send_sems, recv_sems):
        x, y, c = _me()
        me = 4 * x + 2 * y + c
        land_ref[me] = x_ref[...]
        peers = []
        for k in range(1, N_DEV):
            px = 1 - x if k & 4 else x
            py = 1 - y if k & 2 else y
            pc = 1 - c if k & 1 else c
            peers.append((px, py, pc))
        cps = [_remote(x_ref, land_ref.at[me], send_sems.at[k], recv_sems.at[k], peer)
               for k, peer in enumerate(peers)]
        for cp in cps:
            cp.start()
        for k, (px, py, pc) in enumerate(peers):
            _remote(x_ref, land_ref.at[4 * px + 2 * py + pc], send_sems.at[k], recv_sems.at[k],
                    (px, py, pc)).wait_recv()
        for cp in cps:
            cp.wait_send()
        acc = land_ref[0]
        for q in range(1, N_DEV):
            acc = acc + land_ref[q]
        o_ref[...] = acc

    vmem = pl.BlockSpec(memory_space=pltpu.VMEM)
    return pl.pallas_call(
        body, name=name, out_shape=jax.ShapeDtypeStruct((r, n), F32), in_specs=[vmem], out_specs=vmem,
        scratch_shapes=[pltpu.VMEM((N_DEV, r, n), F32), pltpu.SemaphoreType.DMA((N_DEV - 1,)),
                        pltpu.SemaphoreType.DMA((N_DEV - 1,))],
        compiler_params=pltpu.CompilerParams(vmem_limit_bytes=VMEM_LIMIT_BYTES),
    )(buf)


def _small_layout(shapes):
    offs, off = {}, 0
    for name in SMALL:
        n = int(math.prod(shapes[name]))
        offs[name] = (off, n)
        off += n
    rows = -(-off // (SUBLANES * LANES)) * SUBLANES
    return offs, rows


def _pack_small(vals, offs, rows):
    parts = []
    for name in SMALL:
        off, n = offs[name]
        parts.append(vals[name].reshape(n).astype(F32) if name in vals else jnp.zeros((n,), F32))
    used = sum(p.shape[0] for p in parts)
    parts.append(jnp.zeros((rows * LANES - used,), F32))
    return jnp.concatenate(parts).reshape(rows, LANES)


def _unpack_small(buf, offs, shapes):
    flat = buf.reshape(-1)
    return {name: flat[offs[name][0]:offs[name][0] + offs[name][1]].reshape(shapes[name]) for name in SMALL}


def kernel(x, p, norm_mix_g, norm_mlp_g, norm_ple_g, norm_f_g, w_in_rec, conv_a_w, conv_a_b, ln_a_g, ln_a_b, conv_b_w, conv_b_b, w_rg_a, b_rg_a, w_rg_x, b_rg_x, rg_lambda, w_out_rec, w_qkv, w_o_attn, w_mlp_up, w_mlp_down, w_ple_proj, w_ple_gate, loss_target, m_norm_mix_g, m_norm_mlp_g, m_norm_ple_g, m_norm_f_g, m_w_in_rec, m_conv_a_w, m_conv_a_b, m_ln_a_g, m_ln_a_b, m_conv_b_w, m_conv_b_b, m_w_rg_a, m_b_rg_a, m_w_rg_x, m_b_rg_x, m_rg_lambda, m_w_out_rec, m_w_qkv, m_w_o_attn, m_w_mlp_up, m_w_mlp_down, m_w_ple_proj, m_w_ple_gate, v_norm_mix_g, v_norm_mlp_g, v_norm_ple_g, v_norm_f_g, v_w_in_rec, v_conv_a_w, v_conv_a_b, v_ln_a_g, v_ln_a_b, v_conv_b_w, v_conv_b_b, v_w_rg_a, v_b_rg_a, v_w_rg_x, v_b_rg_x, v_rg_lambda, v_w_out_rec, v_w_qkv, v_w_o_attn, v_w_mlp_up, v_w_mlp_down, v_w_ple_proj, v_w_ple_gate):
    wts = dict(norm_mix_g=norm_mix_g, norm_mlp_g=norm_mlp_g, norm_ple_g=norm_ple_g, norm_f_g=norm_f_g,
               w_in_rec=w_in_rec, conv_a_w=conv_a_w, conv_a_b=conv_a_b, ln_a_g=ln_a_g, ln_a_b=ln_a_b,
               conv_b_w=conv_b_w, conv_b_b=conv_b_b, w_rg_a=w_rg_a, b_rg_a=b_rg_a, w_rg_x=w_rg_x, b_rg_x=b_rg_x,
               rg_lambda=rg_lambda, w_out_rec=w_out_rec, w_qkv=w_qkv, w_o_attn=w_o_attn, w_mlp_up=w_mlp_up,
               w_mlp_down=w_mlp_down, w_ple_proj=w_ple_proj, w_ple_gate=w_ple_gate)
    mom = dict(norm_mix_g=m_norm_mix_g, norm_mlp_g=m_norm_mlp_g, norm_ple_g=m_norm_ple_g, norm_f_g=m_norm_f_g,
               w_in_rec=m_w_in_rec, conv_a_w=m_conv_a_w, conv_a_b=m_conv_a_b, ln_a_g=m_ln_a_g, ln_a_b=m_ln_a_b,
               conv_b_w=m_conv_b_w, conv_b_b=m_conv_b_b, w_rg_a=m_w_rg_a, b_rg_a=m_b_rg_a, w_rg_x=m_w_rg_x,
               b_rg_x=m_b_rg_x, rg_lambda=m_rg_lambda, w_out_rec=m_w_out_rec, w_qkv=m_w_qkv, w_o_attn=m_w_o_attn,
               w_mlp_up=m_w_mlp_up, w_mlp_down=m_w_mlp_down, w_ple_proj=m_w_ple_proj, w_ple_gate=m_w_ple_gate)
    var = dict(norm_mix_g=v_norm_mix_g, norm_mlp_g=v_norm_mlp_g, norm_ple_g=v_norm_ple_g, norm_f_g=v_norm_f_g,
               w_in_rec=v_w_in_rec, conv_a_w=v_conv_a_w, conv_a_b=v_conv_a_b, ln_a_g=v_ln_a_g, ln_a_b=v_ln_a_b,
               conv_b_w=v_conv_b_w, conv_b_b=v_conv_b_b, w_rg_a=v_w_rg_a, b_rg_a=v_b_rg_a, w_rg_x=v_w_rg_x,
               b_rg_x=v_b_rg_x, rg_lambda=v_rg_lambda, w_out_rec=v_w_out_rec, w_qkv=v_w_qkv, w_o_attn=v_w_o_attn,
               w_mlp_up=v_w_mlp_up, w_mlp_down=v_w_mlp_down, w_ple_proj=v_w_ple_proj, w_ple_gate=v_w_ple_gate)

    xi, yi, ci = lax.axis_index("x"), lax.axis_index("y"), lax.axis_index("c")
    chip = 2 * xi + yi
    seq, dm = x.shape[1], x.shape[2]
    depth = norm_mix_g.shape[0]
    dh2 = dm // 2
    hd = dh2 // RG_HEADS
    kw_a, kw_b = conv_a_w.shape[1], conv_b_w.shape[1]
    cshard = conv_a_w.shape[2]

    small_shapes = {n: tuple(wts[n].shape) for n in SMALL}
    small_shapes["conv_a_w"] = (conv_a_w.shape[0], kw_a, dh2)
    small_shapes["conv_b_w"] = (conv_b_w.shape[0], kw_b, dh2)
    offs, small_rows = _small_layout(small_shapes)
    south = (ci == 0).astype(F32)
    placed = {}
    for n in ("conv_a_w", "conv_b_w"):
        placed[n] = lax.dynamic_update_slice(jnp.zeros(small_shapes[n], F32), wts[n] * south,
                                             (0, 0, chip * cshard))
    conv_full = _unpack_small(_small_allreduce("small_ar_conv", _pack_small(placed, offs, small_rows)),
                              offs, small_shapes)
    conv_a_full, conv_b_full = conv_full["conv_a_w"], conv_full["conv_b_w"]

    def layer_mats(i):
        j = i // 2
        mix = ["w_in_rec", "w_out_rec"] if i % 2 == 0 else ["w_qkv", "w_o_attn"]
        names = mix + ["w_mlp_up", "w_mlp_down", "w_ple_proj", "w_ple_gate"]
        return [(n, j if n in mix else i) for n in names]

    core = ci.astype(jnp.int32).reshape(1)

    def gather_layer(i):
        mats = layer_mats(i)
        fulls = _all_gather_shards(f"ag_l{i}", [wts[n][idx].astype(BF) for n, idx in mats])
        out = {}
        for (n, idx), full in zip(mats, fulls):
            kk, nn = wts[n].shape[1:]
            out[n] = full if n in COL_SHARDED else full.reshape(1, N_CHIPS * kk, nn)
        return out

    def reduce_layer(i, grads):
        mats = layer_mats(i)
        parts = []
        for n, _ in mats:
            kk, nn = wts[n].shape[1:]
            parts.append(grads[n].reshape(N_CHIPS, 2, kk // 2, nn))
        theirs = _pair_swap(f"rs_pair_l{i}", parts)
        chip_sums = [_add_half(f"rs_add_{n}_l{i}", g, t, core) for (n, _), g, t in zip(mats, parts, theirs)]
        landed = _chip_all_to_all(f"rs_a2a_l{i}", chip_sums)
        halves = [_sum_chips(f"rs_sum_{n}_l{i}", a) for (n, _), a in zip(mats, landed)]
        both = _pair_gather(f"rs_gather_l{i}", halves)
        return {n: b.reshape(wts[n].shape[1:]) for (n, _), b in zip(mats, both)}

    row2 = lambda a: a.reshape(1, -1)
    to3 = lambda a: a.reshape(seq, RG_HEADS, hd)
    to2 = lambda a: a.reshape(seq, dh2)

    h = x[0]
    saved = []
    for i in range(depth):
        j = i // 2
        w = gather_layer(i)
        sv = dict(w=w, h=h)
        (hn,) = _rows_fwd(f"norm_mix_l{i}", f_norm, [h], [row2(norm_mix_g[i])], [((dm,), BF)])
        sv["hn"] = hn
        if i % 2 == 0:
            u = _mm_nn(f"in_rec_l{i}", hn, w["w_in_rec"], F32)
            (v,) = _rows_fwd(f"glu_l{i}", f_glu, [Cols(u, 0, dh2), Cols(u, 1, dh2)], [], [((dh2,), F32)])
            v3 = to3(v)
            yc = to2(_conv3(f"conv_a_l{i}", v3, conv_a_full[j].reshape(kw_a, RG_HEADS, hd),
                            conv_a_b[j].reshape(RG_HEADS, hd), True))
            (ya,) = _rows_fwd(f"ln_silu_l{i}", f_ln_silu, [yc], [row2(ln_a_g[j]), row2(ln_a_b[j])], [((dh2,), BF)])
            xr3 = to3(u[:, 2 * dh2:3 * dh2])
            xc = to2(_conv3(f"conv_b_l{i}", xr3, conv_b_full[j].reshape(kw_b, RG_HEADS, hd),
                            conv_b_b[j].reshape(RG_HEADS, hd), True))
            ra = _bd_nn(f"rg_a_l{i}", xc, w_rg_a[j])
            ix = _bd_nn(f"rg_x_l{i}", xc, w_rg_x[j])
            gate_params = [row2(b_rg_a[j]), row2(b_rg_x[j]), row2(rg_lambda[j])]
            a, uu = _rows_fwd(f"gates_l{i}", f_gates, [ra, ix, xc], gate_params, [((dh2,), F32), ((dh2,), F32)])
            hs3 = _scan_fwd(f"scan_l{i}", to3(a), to3(uu))
            hs = to2(hs3)
            (yb,) = _rows_fwd(f"gelu_gate_l{i}", f_gelu_gate, [hs, Cols(u, 3, dh2)], [], [((dh2,), BF)])
            cat = jnp.concatenate([ya, yb], axis=1)
            mix = _mm_nn(f"out_rec_l{i}", cat, w["w_out_rec"], F32)
            sv.update(u=u, v3=v3, yc=yc, xr3=xr3, xc=xc, ra=ra, ix=ix, a=a, hs3=hs3, hs=hs, cat=cat,
                      gate_params=gate_params)
        else:
            qkv = _mm_nn(f"qkv_l{i}", hn, w["w_qkv"], BF)
            o, lam = _attn_fwd(f"attn_l{i}", qkv, SB_HEADS)
            mix = _mm_nn(f"o_attn_l{i}", o, w["w_o_attn"], F32)
            sv.update(qkv=qkv, o=o, lam=lam)
        h1, hn2 = _rows_fwd(f"norm_mlp_l{i}", f_add_norm, [h, mix], [row2(norm_mlp_g[i])], [((dm,), F32), ((dm,), BF)])
        up = _mm_nn(f"mlp_up_l{i}", hn2, w["w_mlp_up"], F32)
        (act,) = _rows_fwd(f"relu2_l{i}", f_relu2, [up], [], [((up.shape[1],), BF)])
        mlp = _mm_nn(f"mlp_down_l{i}", act, w["w_mlp_down"], F32)
        h2, hn3 = _rows_fwd(f"norm_ple_l{i}", f_add_norm, [h1, mlp], [row2(norm_ple_g[i])], [((dm,), F32), ((dm,), BF)])
        gpre = _mm_nn(f"ple_gate_l{i}", hn3, w["w_ple_gate"], F32)
        pb = p[i, 0].astype(BF)
        pp = _mm_nn(f"ple_proj_l{i}", pb, w["w_ple_proj"], F32)
        (h3,) = _rows_fwd(f"ple_l{i}", f_ple, [h2, pp, gpre], [], [((dm,), F32)])
        sv.update(mix=mix, h1=h1, hn2=hn2, up=up, act=act, mlp=mlp, h2=h2, hn3=hn3, gpre=gpre, pb=pb, pp=pp)
        saved.append(sv)
        h = h3

    loss_vec, dh, g_norm_f = _loss_and_grad("loss_head", h, loss_target[0], row2(norm_f_g))
    loss = lax.psum(loss_vec[0, 0], ("x", "y", "c"))

    big_grads = {n: [None] * wts[n].shape[0] for n in BIG}
    small_grads = {n: [None] * wts[n].shape[0] for n in SMALL if n != "norm_f_g"}
    for i in reversed(range(depth)):
        j = i // 2
        sv = saved[i]
        w = sv["w"]
        lg = {}
        (d_h2, d_pp, d_gpre), _ = _rows_bwd(f"ple_bwd_l{i}", f_ple, [sv["h2"], sv["pp"], sv["gpre"]], [], [dh],
                                            [F32, BF, BF])
        lg["w_ple_proj"] = _mm_tn(f"ple_proj_dw_l{i}", sv["pb"], d_pp, N_CHIPS)
        lg["w_ple_gate"] = _mm_tn(f"ple_gate_dw_l{i}", sv["hn3"], d_gpre, 1)
        d_hn3 = _mm_nt(f"ple_gate_dx_l{i}", d_gpre, w["w_ple_gate"], F32)
        (d_h1, d_mlp), (g_ple,) = _rows_bwd(f"norm_ple_bwd_l{i}", f_add_norm, [sv["h1"], sv["mlp"]],
                                            [row2(norm_ple_g[i])], [d_h2, d_hn3], [F32, BF])
        lg["w_mlp_down"] = _mm_tn(f"mlp_down_dw_l{i}", sv["act"], d_mlp, 1)
        d_act = _mm_nt(f"mlp_down_dx_l{i}", d_mlp, w["w_mlp_down"], F32)
        (d_up,), _ = _rows_bwd(f"relu2_bwd_l{i}", f_relu2, [sv["up"]], [], [d_act], [BF])
        lg["w_mlp_up"] = _mm_tn(f"mlp_up_dw_l{i}", sv["hn2"], d_up, N_CHIPS)
        d_hn2 = _mm_nt(f"mlp_up_dx_l{i}", d_up, w["w_mlp_up"], F32)
        (d_h0, d_mix), (g_mlp,) = _rows_bwd(f"norm_mlp_bwd_l{i}", f_add_norm, [sv["h"], sv["mix"]],
                                            [row2(norm_mlp_g[i])], [d_h1, d_hn2], [F32, BF])
        if i % 2 == 0:
            u = sv["u"]
            lg["w_out_rec"] = _mm_tn(f"out_rec_dw_l{i}", sv["cat"], d_mix, 1)
            d_cat = _mm_nt(f"out_rec_dx_l{i}", d_mix, w["w_out_rec"], F32)
            (d_yc,), (g_ln_g, g_ln_b) = _rows_bwd(f"ln_silu_bwd_l{i}", f_ln_silu, [sv["yc"]],
                                                  [row2(ln_a_g[j]), row2(ln_a_b[j])], [Cols(d_cat, 0, dh2)], [F32])
            d_yc3 = to3(d_yc)
            wa3 = conv_a_full[j].reshape(kw_a, RG_HEADS, hd)
            d_v = to2(_conv3(f"conv_a_dx_l{i}", d_yc3, wa3, None, False))
            g_ca_w, g_ca_b = _conv3_bwd_w(f"conv_a_dw_l{i}", d_yc3, sv["v3"], kw_a)
            (d_aval, d_agate), _ = _rows_bwd(f"glu_bwd_l{i}", f_glu, [Cols(u, 0, dh2), Cols(u, 1, dh2)], [], [d_v],
                                             [BF, BF])
            (d_hs, d_gr), _ = _rows_bwd(f"gelu_gate_bwd_l{i}", f_gelu_gate, [sv["hs"], Cols(u, 3, dh2)], [],
                                        [Cols(d_cat, 1, dh2)], [F32, BF])
            hprev3 = jnp.concatenate([jnp.zeros((1, RG_HEADS, hd), F32), sv["hs3"][:-1]], axis=0)
            da3, du3 = _scan_bwd(f"scan_bwd_l{i}", to3(sv["a"]), to3(d_hs), hprev3)
            (d_ra, d_ix, d_xc0), (g_ba, g_bx, g_lam) = _rows_bwd(
                f"gates_bwd_l{i}", f_gates, [sv["ra"], sv["ix"], sv["xc"]], sv["gate_params"],
                [to2(da3), to2(du3)], [BF, BF, F32])
            g_wa = _bd_tn(f"rg_a_dw_l{i}", sv["xc"], d_ra, RG_HEADS)
            g_wx = _bd_tn(f"rg_x_dw_l{i}", sv["xc"], d_ix, RG_HEADS)
            d_xc = _bd_nt2(f"rg_dx_l{i}", d_xc0, d_ra, w_rg_a[j], d_ix, w_rg_x[j])
            d_xc3 = to3(d_xc)
            wb3 = conv_b_full[j].reshape(kw_b, RG_HEADS, hd)
            d_xr = to2(_conv3(f"conv_b_dx_l{i}", d_xc3, wb3, None, False))
            g_cb_w, g_cb_b = _conv3_bwd_w(f"conv_b_dw_l{i}", d_xc3, sv["xr3"], kw_b)
            d_u = jnp.concatenate([d_aval, d_agate, d_xr.astype(BF), d_gr], axis=1)
            lg["w_in_rec"] = _mm_tn(f"in_rec_dw_l{i}", sv["hn"], d_u, N_CHIPS)
            d_hn = _mm_nt(f"in_rec_dx_l{i}", d_u, w["w_in_rec"], F32)
            small_grads["conv_a_w"][j] = g_ca_w.reshape(kw_a, dh2)
            small_grads["conv_a_b"][j] = g_ca_b.reshape(dh2)
            small_grads["ln_a_g"][j] = g_ln_g.reshape(dh2)
            small_grads["ln_a_b"][j] = g_ln_b.reshape(dh2)
            small_grads["conv_b_w"][j] = g_cb_w.reshape(kw_b, dh2)
            small_grads["conv_b_b"][j] = g_cb_b.reshape(dh2)
            small_grads["w_rg_a"][j] = g_wa
            small_grads["b_rg_a"][j] = g_ba.reshape(dh2)
            small_grads["w_rg_x"][j] = g_wx
            small_grads["b_rg_x"][j] = g_bx.reshape(dh2)
            small_grads["rg_lambda"][j] = g_lam.reshape(dh2)
        else:
            lg["w_o_attn"] = _mm_tn(f"o_attn_dw_l{i}", sv["o"], d_mix, 1)
            d_o = _mm_nt(f"o_attn_dx_l{i}", d_mix, w["w_o_attn"], BF)
            dq, dk, dv = _attn_bwd(f"attn_bwd_l{i}", sv["qkv"], sv["lam"], d_o, SB_HEADS)
            d_qkv = jnp.concatenate([dq, dk, dv], axis=1).astype(BF)
            lg["w_qkv"] = _mm_tn(f"qkv_dw_l{i}", sv["hn"], d_qkv, N_CHIPS)
            d_hn = _mm_nt(f"qkv_dx_l{i}", d_qkv, w["w_qkv"], F32)
        (dh,), (g_mix,) = _rows_bwd(f"norm_mix_bwd_l{i}", f_norm, [sv["h"]], [row2(norm_mix_g[i])], [d_hn], [F32],
                                    addend=d_h0)
        small_grads["norm_mix_g"][i] = g_mix.reshape(dm)
        small_grads["norm_mlp_g"][i] = g_mlp.reshape(dm)
        small_grads["norm_ple_g"][i] = g_ple.reshape(dm)
        reduced = reduce_layer(i, lg)
        for n, idx in layer_mats(i):
            big_grads[n][idx] = reduced[n]
    grad_x = dh[None]

    small_vals = {n: jnp.stack(small_grads[n]) for n in small_grads}
    small_vals["norm_f_g"] = g_norm_f.reshape(dm)
    reduced_small = _unpack_small(_small_allreduce("small_ar_grads", _pack_small(small_vals, offs, small_rows)),
                                  offs, small_shapes)
    grads = {}
    for n in SMALL:
        g = reduced_small[n]
        if n in ("conv_a_w", "conv_b_w"):
            g = lax.dynamic_slice_in_dim(g, chip * cshard, cshard, axis=2)
        grads[n] = g
    for n in BIG:
        grads[n] = jnp.stack(big_grads[n])

    delta, new_m, new_v = {}, {}, {}
    for n in WEIGHTS:
        delta[n], new_m[n], new_v[n] = _adamw(f"adamw_{n}", wts[n], grads[n], mom[n], var[n])
    return (loss, grad_x, *[grads[n] for n in WEIGHTS], *[delta[n] for n in WEIGHTS],
            *[new_m[n] for n in WEIGHTS], *[new_v[n] for n in WEIGHTS])
```

```python
import math

import jax
import jax.numpy as jnp
from jax import lax
from jax.experimental import pallas as pl
from jax.experimental.pallas import tpu as pltpu

F32 = jnp.float32
BF = jnp.bfloat16
MESH = pl.DeviceIdType.MESH

VMEM_LIMIT_BYTES = 56 * 1024 * 1024
LANES = 128
SUBLANES = 8
N_CHIPS = 4
N_DEV = 8

EPS = 1e-6
SB_HEADS = 16
RG_HEADS = 8
RG_C = 8.0
ADAM_LR = 0.001
ADAM_B1 = 0.9
ADAM_B2 = 0.999
ADAM_EPS = 1e-08
ADAM_WD = 0.01
ADAM_STEP = 10

BIG = ("w_in_rec", "w_out_rec", "w_qkv", "w_o_attn", "w_mlp_up", "w_mlp_down", "w_ple_proj", "w_ple_gate")
COL_SHARDED = ("w_in_rec", "w_qkv", "w_mlp_up", "w_ple_proj")
SMALL = ("norm_mix_g", "norm_mlp_g", "norm_ple_g", "norm_f_g", "conv_a_w", "conv_a_b", "ln_a_g", "ln_a_b",
         "conv_b_w", "conv_b_b", "w_rg_a", "b_rg_a", "w_rg_x", "b_rg_x", "rg_lambda")
WEIGHTS = ("norm_mix_g", "norm_mlp_g", "norm_ple_g", "norm_f_g", "w_in_rec", "conv_a_w", "conv_a_b", "ln_a_g",
           "ln_a_b", "conv_b_w", "conv_b_b", "w_rg_a", "b_rg_a", "w_rg_x", "b_rg_x", "rg_lambda", "w_out_rec",
           "w_qkv", "w_o_attn", "w_mlp_up", "w_mlp_down", "w_ple_proj", "w_ple_gate")


def _params(sem):
    return pltpu.CompilerParams(dimension_semantics=sem, vmem_limit_bytes=VMEM_LIMIT_BYTES)


def _pow2_floor(n):
    return 1 << (int(n).bit_length() - 1)


def _sig(x):
    return 0.5 * (jnp.tanh(0.5 * x) + 1.0)


def _softplus(x):
    return jnp.maximum(x, 0.0) + jnp.log(1.0 + jnp.exp(-jnp.maximum(x, -x)))


def _rms(x, g):
    return x * lax.rsqrt(jnp.mean(x * x, axis=-1, keepdims=True) + EPS) * g


def _neg_expm1(x):
    series = -x * (1.0 + 0.5 * x * (1.0 + x * (1.0 / 3.0) * (1.0 + 0.25 * x)))
    return jnp.where(x > -1e-2, series, 1.0 - jnp.exp(x))


def f_norm(h, g):
    return (_rms(h, g),)


def f_add_norm(h, mix, g):
    h1 = h + mix
    return h1, _rms(h1, g)


def f_ple(h, pp, gpre):
    return (h + pp * _sig(gpre),)


def f_relu2(u):
    r = jnp.maximum(u, 0.0)
    return (r * r,)


def f_glu(a, b):
    return (a * _sig(b),)


def f_ln_silu(x, g, b):
    mu = jnp.mean(x, axis=-1, keepdims=True)
    xc = x - mu
    var = jnp.mean(xc * xc, axis=-1, keepdims=True)
    y = xc * lax.rsqrt(var + EPS) * g + b
    return (y * _sig(y),)


def f_gelu_gate(hs, gr):
    inner = math.sqrt(2.0 / math.pi) * (gr + 0.044715 * gr * gr * gr)
    return (hs * (0.5 * gr * (1.0 + jnp.tanh(inner))),)


def f_gates(ra, ix, xc, b_a, b_x, lam):
    r = _sig(ra + b_a)
    i = _sig(ix + b_x)
    log_a = -RG_C * r * _softplus(-lam)
    a = jnp.exp(log_a)
    mult = jnp.sqrt(_neg_expm1(2.0 * log_a))
    return a, mult * (i * xc)


class Cols:
    def __init__(self, arr, blk, width):
        self.arr, self.blk, self.width = arr, blk, width
        self.shape = (arr.shape[0], width)
        self.dtype = arr.dtype


def _row_spec(a, t):
    if isinstance(a, Cols):
        blk = a.blk
        return pl.BlockSpec((t, a.width), lambda i: (i, blk))
    nd = len(a.shape)
    return pl.BlockSpec((t,) + tuple(a.shape[1:]), lambda i: (i,) + (0,) * (nd - 1))


def _full_spec(a):
    nd = len(a.shape)
    return pl.BlockSpec(tuple(a.shape), lambda i: (0,) * nd)


def _arr(a):
    return a.arr if isinstance(a, Cols) else a


def _row_tile(shapes):
    s = shapes[0][0]
    widest = max(int(math.prod(sh[1:])) for sh in shapes)
    t = _pow2_floor(max(16, (1 << 18) // widest))
    t = min(t, s)
    assert s % t == 0
    return t


def _rows_fwd(name, fn, rows, params, outs):
    s = rows[0].shape[0]
    t = _row_tile([r.shape for r in rows] + [(s,) + tuple(o[0]) for o in outs])
    nin = len(rows) + len(params)

    def body(*refs):
        vals = [r[...].astype(F32) for r in refs[:nin]]
        res = fn(*vals)
        for o_ref, v in zip(refs[nin:], res):
            o_ref[...] = v.astype(o_ref.dtype)

    out_shape = [jax.ShapeDtypeStruct((s,) + tuple(o[0]), o[1]) for o in outs]
    res = pl.pallas_call(
        body, grid=(s // t,), name=name,
        in_specs=[_row_spec(r, t) for r in rows] + [_full_spec(p) for p in params],
        out_specs=[_row_spec(o, t) for o in out_shape], out_shape=out_shape,
        compiler_params=_params(("parallel",)),
    )(*[_arr(r) for r in rows], *params)
    return res


def _rows_bwd(name, fn, rows, params, cots, out_dtypes, addend=None):
    s = rows[0].shape[0]
    nr, npar, nc = len(rows), len(params), len(cots)
    extra = [addend] if addend is not None else []
    t = _row_tile([r.shape for r in rows] + [c.shape for c in cots])

    def body(*refs):
        rs = [r[...].astype(F32) for r in refs[:nr]]
        ps = [r[...].astype(F32) for r in refs[nr:nr + npar]]
        cs = tuple(r[...].astype(F32) for r in refs[nr + npar:nr + npar + nc])
        k = nr + npar + nc
        ad = refs[k][...].astype(F32) if extra else None
        k += len(extra)
        grow = refs[k:k + nr]
        gpar = refs[k + nr:]
        _, vjp = jax.vjp(fn, *rs, *ps)
        g = vjp(cs)
        for j in range(nr):
            val = g[j]
            if j == 0 and ad is not None:
                val = val + ad
            grow[j][...] = val.astype(grow[j].dtype)
        first = pl.program_id(0) == 0
        for j in range(npar):
            @pl.when(first)
            def _(j=j):
                gpar[j][...] = jnp.zeros_like(gpar[j])
            gpar[j][...] += g[nr + j]

    out_shape = ([jax.ShapeDtypeStruct(tuple(r.shape), dt) for r, dt in zip(rows, out_dtypes)]
                 + [jax.ShapeDtypeStruct(tuple(p.shape), F32) for p in params])
    res = pl.pallas_call(
        body, grid=(s // t,), name=name,
        in_specs=([_row_spec(r, t) for r in rows] + [_full_spec(p) for p in params]
                  + [_row_spec(c, t) for c in cots] + [_row_spec(a, t) for a in extra]),
        out_specs=([_row_spec(o, t) for o in out_shape[:nr]] + [_full_spec(o) for o in out_shape[nr:]]),
        out_shape=out_shape,
        compiler_params=_params(("arbitrary",)),
    )(*[_arr(r) for r in rows], *params, *[_arr(c) for c in cots], *extra)
    return list(res[:nr]), list(res[nr:])


def _loss_and_grad(name, h, tgt, g):
    s, d = h.shape
    t = _row_tile([h.shape])

    def body(h_ref, t_ref, g_ref, loss_ref, dh_ref, dg_ref):
        tg = t_ref[...]

        def f(hh, gg):
            e = _rms(hh, gg) - tg
            return 0.5 * jnp.mean(e * e, axis=-1, keepdims=True)

        val, vjp = jax.vjp(f, h_ref[...], g_ref[...])
        dh, dg = vjp(jnp.ones_like(val))
        dh_ref[...] = dh

        @pl.when(pl.program_id(0) == 0)
        def _():
            loss_ref[...] = jnp.zeros_like(loss_ref)
            dg_ref[...] = jnp.zeros_like(dg_ref)
        loss_ref[...] += jnp.broadcast_to(jnp.sum(val, axis=0, keepdims=True), loss_ref.shape)
        dg_ref[...] += dg

    return pl.pallas_call(
        body, grid=(s // t,), name=name,
        in_specs=[_row_spec(h, t), _row_spec(tgt, t), _full_spec(g)],
        out_specs=[pl.BlockSpec((1, LANES), lambda i: (0, 0)), _row_spec(h, t), _full_spec(g)],
        out_shape=[jax.ShapeDtypeStruct((1, LANES), F32), jax.ShapeDtypeStruct((s, d), F32),
                   jax.ShapeDtypeStruct(tuple(g.shape), F32)],
        compiler_params=_params(("arbitrary",)),
    )(h, tgt, g)


def _col_tile(nc):
    for t in (1024, 768, 512):
        if nc >= t and nc % t == 0:
            return t
    return nc


def _mm(name, a, b, grid, in_specs, out_spec, out_shape, dims, acc_shape):
    nsteps = grid[2]

    def body(a_ref, b_ref, o_ref, acc_ref):
        k = pl.program_id(2)

        @pl.when(k == 0)
        def _():
            acc_ref[...] = jnp.zeros_like(acc_ref)
        acc_ref[...] += lax.dot_general(a_ref[...].astype(BF), b_ref[...].astype(BF), dims,
                                        preferred_element_type=F32)

        @pl.when(k == nsteps - 1)
        def _():
            o_ref[...] = acc_ref[...].astype(o_ref.dtype)

    return pl.pallas_call(
        body, grid=grid, name=name, in_specs=in_specs, out_specs=out_spec, out_shape=out_shape,
        scratch_shapes=[pltpu.VMEM(acc_shape, F32)],
        compiler_params=_params(("parallel", "parallel", "arbitrary")),
    )(a, b)


def _mm_nn(name, a, w, out_dtype):
    m, k = a.shape
    g, k2, nc = w.shape
    assert k == k2
    tm, tk, tn = min(m, 1024), min(k, 1024), _col_tile(nc)
    r = nc // tn
    return _mm(name, a, w, (m // tm, g * r, k // tk),
               [pl.BlockSpec((tm, tk), lambda i, j, l: (i, l)),
                pl.BlockSpec((None, tk, tn), lambda i, j, l: (j // r, l, j % r))],
               pl.BlockSpec((tm, tn), lambda i, j, l: (i, j)),
               jax.ShapeDtypeStruct((m, g * nc), out_dtype), (((1,), (0,)), ((), ())), (tm, tn))


def _mm_nt(name, a, w, out_dtype):
    m, n = a.shape
    g, k, nc = w.shape
    assert n == g * nc
    tm, tk, tn = min(m, 1024), min(k, 1024), _col_tile(nc)
    r = nc // tn
    return _mm(name, a, w, (m // tm, k // tk, g * r),
               [pl.BlockSpec((tm, tn), lambda i, j, l: (i, l)),
                pl.BlockSpec((None, tk, tn), lambda i, j, l: (l // r, j, l % r))],
               pl.BlockSpec((tm, tk), lambda i, j, l: (i, j)),
               jax.ShapeDtypeStruct((m, k), out_dtype), (((1,), (1,)), ((), ())), (tm, tk))


def _mm_tn(name, a, b, g):
    m, k = a.shape
    m2, n = b.shape
    assert m == m2 and n % g == 0
    nc = n // g
    tm, tk, tn = min(m, 1024), min(k, 1024), _col_tile(nc)
    r = nc // tn
    return _mm(name, a, b, (k // tk, g * r, m // tm),
               [pl.BlockSpec((tm, tk), lambda i, j, l: (l, i)),
                pl.BlockSpec((tm, tn), lambda i, j, l: (l, j))],
               pl.BlockSpec((None, tk, tn), lambda i, j, l: (j // r, i, j % r)),
               jax.ShapeDtypeStruct((g, k, nc), F32), (((0,), (0,)), ((), ())), (tk, tn))


def _bd_nn(name, x, w):
    s, c = x.shape
    nh, hd, _ = w.shape
    tm = min(s, 1024)

    def body(x_ref, w_ref, o_ref):
        o_ref[...] = jnp.dot(x_ref[...].astype(BF), w_ref[...].astype(BF), preferred_element_type=F32)

    return pl.pallas_call(
        body, grid=(s // tm, nh), name=name,
        in_specs=[pl.BlockSpec((tm, hd), lambda i, h: (i, h)), pl.BlockSpec((None, hd, hd), lambda i, h: (h, 0, 0))],
        out_specs=pl.BlockSpec((tm, hd), lambda i, h: (i, h)),
        out_shape=jax.ShapeDtypeStruct((s, c), F32),
        compiler_params=_params(("parallel", "parallel")),
    )(x, w)


def _bd_nt2(name, add, dy1, w1, dy2, w2):
    s, c = add.shape
    nh, hd, _ = w1.shape
    tm = min(s, 1024)
    nt = (((1,), (1,)), ((), ()))

    def body(a_ref, d1_ref, w1_ref, d2_ref, w2_ref, o_ref):
        o_ref[...] = (a_ref[...]
                      + lax.dot_general(d1_ref[...].astype(BF), w1_ref[...].astype(BF), nt, preferred_element_type=F32)
                      + lax.dot_general(d2_ref[...].astype(BF), w2_ref[...].astype(BF), nt, preferred_element_type=F32))

    row = pl.BlockSpec((tm, hd), lambda i, h: (i, h))
    wsp = pl.BlockSpec((None, hd, hd), lambda i, h: (h, 0, 0))
    return pl.pallas_call(
        body, grid=(s // tm, nh), name=name, in_specs=[row, row, wsp, row, wsp], out_specs=row,
        out_shape=jax.ShapeDtypeStruct((s, c), F32), compiler_params=_params(("parallel", "parallel")),
    )(add, dy1, w1, dy2, w2)


def _bd_tn(name, x, dy, nh):
    s, c = x.shape
    hd = c // nh
    tm = min(s, 1024)
    tn = (((0,), (0,)), ((), ()))

    def body(x_ref, d_ref, o_ref):
        @pl.when(pl.program_id(1) == 0)
        def _():
            o_ref[...] = jnp.zeros_like(o_ref)
        o_ref[...] += lax.dot_general(x_ref[...].astype(BF), d_ref[...].astype(BF), tn, preferred_element_type=F32)

    row = pl.BlockSpec((tm, hd), lambda h, i: (i, h))
    return pl.pallas_call(
        body, grid=(nh, s // tm), name=name, in_specs=[row, row],
        out_specs=pl.BlockSpec((None, hd, hd), lambda h, i: (h, 0, 0)),
        out_shape=jax.ShapeDtypeStruct((nh, hd, hd), F32), compiler_params=_params(("parallel", "arbitrary")),
    )(x, dy)


CONV_HALO = 32
CONV_SUB = 16


def _conv3(name, x3, w3, b2, causal):
    s, sl, hd = x3.shape
    kw = w3.shape[0]
    tc = min(s, 256)
    nchunks = s // tc
    per = tc // CONV_HALO
    nhalo = s // CONV_HALO
    assert kw - 1 <= CONV_HALO and tc % CONV_HALO == 0 and tc % CONV_SUB == 0
    has_bias = b2 is not None

    def body(*refs):
        if has_bias:
            cur_ref, halo_ref, w_ref, b_ref, y_ref, win_ref = refs
        else:
            cur_ref, halo_ref, w_ref, y_ref, win_ref = refs
        i = pl.program_id(0)
        if causal:
            win_ref[0:CONV_HALO] = jnp.where(i > 0, halo_ref[...], 0.0)
            win_ref[CONV_HALO:CONV_HALO + tc] = cur_ref[...]
        else:
            win_ref[0:tc] = cur_ref[...]
            win_ref[tc:tc + CONV_HALO] = jnp.where(i < nchunks - 1, halo_ref[...], 0.0)

        def sub_step(j, carry):
            t0 = pl.multiple_of(j * CONV_SUB, CONV_SUB)
            if has_bias:
                acc = jnp.broadcast_to(b_ref[...], (CONV_SUB, sl, hd))
            else:
                acc = jnp.zeros((CONV_SUB, sl, hd), F32)
            for k in range(kw):
                off = CONV_HALO - (kw - 1) + k if causal else kw - 1 - k
                acc = acc + w_ref[k] * win_ref[pl.ds(t0 + off, CONV_SUB)]
            y_ref[pl.ds(t0, CONV_SUB)] = acc
            return carry

        lax.fori_loop(0, tc // CONV_SUB, sub_step, 0)

    if causal:
        halo_map = lambda i: (jnp.maximum(i * per - 1, 0), 0, 0)
    else:
        halo_map = lambda i: (jnp.minimum((i + 1) * per, nhalo - 1), 0, 0)
    chunk = pl.BlockSpec((tc, sl, hd), lambda i: (i, 0, 0))
    in_specs = [chunk, pl.BlockSpec((CONV_HALO, sl, hd), halo_map), pl.BlockSpec((kw, sl, hd), lambda i: (0, 0, 0))]
    args = [x3, x3, w3]
    if has_bias:
        in_specs.append(pl.BlockSpec((sl, hd), lambda i: (0, 0)))
        args.append(b2)
    return pl.pallas_call(
        body, grid=(nchunks,), name=name, in_specs=in_specs, out_specs=chunk,
        out_shape=jax.ShapeDtypeStruct((s, sl, hd), F32),
        scratch_shapes=[pltpu.VMEM((tc + CONV_HALO, sl, hd), F32)],
        compiler_params=_params(("parallel",)),
    )(*args)


def _conv3_bwd_w(name, dy3, x3, kw):
    s, sl, hd = x3.shape
    tc = min(s, 256)
    per = tc // CONV_HALO

    def body(dy_ref, cur_ref, halo_ref, dw_ref, db_ref, win_ref):
        i = pl.program_id(0)

        @pl.when(i == 0)
        def _():
            dw_ref[...] = jnp.zeros_like(dw_ref)
            db_ref[...] = jnp.zeros_like(db_ref)
        win_ref[0:CONV_HALO] = jnp.where(i > 0, halo_ref[...], 0.0)
        win_ref[CONV_HALO:CONV_HALO + tc] = cur_ref[...]

        def sub_step(j, carry):
            t0 = pl.multiple_of(j * SUBLANES, SUBLANES)
            dy = dy_ref[pl.ds(t0, SUBLANES)]
            new = [carry[k] + jnp.sum(dy * win_ref[pl.ds(t0 + CONV_HALO - (kw - 1) + k, SUBLANES)], axis=0)
                   for k in range(kw)]
            new.append(carry[kw] + jnp.sum(dy, axis=0))
            return tuple(new)

        zero = jnp.zeros((sl, hd), F32)
        res = lax.fori_loop(0, tc // SUBLANES, sub_step, tuple(zero for _ in range(kw + 1)))
        for k in range(kw):
            dw_ref[k] += res[k]
        db_ref[...] += res[kw]

    chunk = pl.BlockSpec((tc, sl, hd), lambda i: (i, 0, 0))
    return pl.pallas_call(
        body, grid=(s // tc,), name=name,
        in_specs=[chunk, chunk, pl.BlockSpec((CONV_HALO, sl, hd), lambda i: (jnp.maximum(i * per - 1, 0), 0, 0))],
        out_specs=[pl.BlockSpec((kw, sl, hd), lambda i: (0, 0, 0)), pl.BlockSpec((sl, hd), lambda i: (0, 0))],
        out_shape=[jax.ShapeDtypeStruct((kw, sl, hd), F32), jax.ShapeDtypeStruct((sl, hd), F32)],
        scratch_shapes=[pltpu.VMEM((tc + CONV_HALO, sl, hd), F32)],
        compiler_params=_params(("arbitrary",)),
    )(dy3, x3, x3)


def _scan_fwd(name, a3, u3):
    s, sl, hd = a3.shape
    tc = min(s, 512)

    def body(a_ref, u_ref, h_ref, carry_ref):
        @pl.when(pl.program_id(0) == 0)
        def _():
            carry_ref[...] = jnp.zeros_like(carry_ref)

        def step(t, h):
            h = a_ref[t] * h + u_ref[t]
            h_ref[t] = h
            return h

        carry_ref[...] = lax.fori_loop(0, tc, step, carry_ref[...], unroll=8)

    chunk = pl.BlockSpec((tc, sl, hd), lambda i: (i, 0, 0))
    return pl.pallas_call(
        body, grid=(s // tc,), name=name, in_specs=[chunk, chunk], out_specs=chunk,
        out_shape=jax.ShapeDtypeStruct((s, sl, hd), F32), scratch_shapes=[pltpu.VMEM((sl, hd), F32)],
        compiler_params=_params(("arbitrary",)),
    )(a3, u3)


def _scan_bwd(name, a3, gh3, hprev3):
    s, sl, hd = a3.shape
    tc = min(s, 512)
    n = s // tc

    def body(a_ref, g_ref, hp_ref, da_ref, du_ref, carry_ref):
        @pl.when(pl.program_id(0) == 0)
        def _():
            carry_ref[...] = jnp.zeros_like(carry_ref)

        def step(j, c):
            t = tc - 1 - j
            lam = g_ref[t] + c
            du_ref[t] = lam
            da_ref[t] = lam * hp_ref[t]
            return a_ref[t] * lam

        carry_ref[...] = lax.fori_loop(0, tc, step, carry_ref[...], unroll=8)

    chunk = pl.BlockSpec((tc, sl, hd), lambda i: (n - 1 - i, 0, 0))
    shp = jax.ShapeDtypeStruct((s, sl, hd), F32)
    return pl.pallas_call(
        body, grid=(n,), name=name, in_specs=[chunk, chunk, chunk], out_specs=[chunk, chunk],
        out_shape=[shp, shp], scratch_shapes=[pltpu.VMEM((sl, hd), F32)],
        compiler_params=_params(("arbitrary",)),
    )(a3, gh3, hprev3)


ATT_Q = 512
ATT_K = 128
NT_DIMS = (((1,), (1,)), ((), ()))
TN_DIMS = (((0,), (0,)), ((), ()))


def _attn_tiles(s):
    tq = min(s, ATT_Q)
    kb = min(tq, ATT_K)
    assert s % tq == 0 and tq % kb == 0
    return tq, kb


def _hilo_dot(x, u):
    hi = x.astype(BF)
    lo = (x - hi.astype(F32)).astype(BF)
    return jnp.dot(hi, u, preferred_element_type=F32) + jnp.dot(lo, u, preferred_element_type=F32)


def _attn_fwd(name, qkv, nh):
    s, d3 = qkv.shape
    d = d3 // 3
    dh = d // nh
    tq, kb = _attn_tiles(s)
    per = tq // kb
    scale = 1.0 / math.sqrt(dh)

    def body(q_ref, k_ref, v_ref, o_ref, lam_ref):
        i = pl.program_id(1)
        q = q_ref[...]
        after = (lax.broadcasted_iota(jnp.int32, (kb, kb), 0)
                 > lax.broadcasted_iota(jnp.int32, (kb, kb), 1)).astype(BF)
        qpos = i * tq + lax.broadcasted_iota(jnp.int32, (tq, kb), 0)
        kcol = lax.broadcasted_iota(jnp.int32, (tq, kb), 1)

        def block(b, carry, masked):
            acc, tail_carry = carry
            off = pl.multiple_of(b * kb, kb)
            kt = k_ref[pl.ds(off, kb), :]
            vt = v_ref[pl.ds(off, kb), :]
            z = lax.dot_general(q, kt, NT_DIMS, preferred_element_type=F32) * scale
            lraw = -_softplus(z)
            if masked:
                mask = (off + kcol) < qpos
                lm = jnp.where(mask, lraw, 0.0)
            else:
                lm = lraw
            w = jnp.exp(z + lraw + _hilo_dot(lm, after) + tail_carry)
            if masked:
                w = jnp.where(mask, w, 0.0)
            acc = acc + jnp.dot(w.astype(BF), vt, preferred_element_type=F32)
            return acc, tail_carry + jnp.sum(lm, axis=1, keepdims=True)

        carry = (jnp.zeros((tq, dh), F32), jnp.zeros((tq, 1), F32))
        for jj in range(per):
            carry = block((i + 1) * per - 1 - jj, carry, True)

        def group(gi, c):
            for jj in range(per):
                c = block((i - gi) * per - 1 - jj, c, False)
            return c

        acc, total = lax.fori_loop(0, i, group, carry)
        o_ref[...] = acc.astype(o_ref.dtype)
        lam_ref[...] = total

    return pl.pallas_call(
        body, grid=(nh, s // tq), name=name,
        in_specs=[pl.BlockSpec((tq, dh), lambda h, i: (i, h)),
                  pl.BlockSpec((s, dh), lambda h, i: (0, nh + h)),
                  pl.BlockSpec((s, dh), lambda h, i: (0, 2 * nh + h))],
        out_specs=[pl.BlockSpec((tq, dh), lambda h, i: (i, h)), pl.BlockSpec((None, tq, 1), lambda h, i: (h, i, 0))],
        out_shape=[jax.ShapeDtypeStruct((s, d), BF), jax.ShapeDtypeStruct((nh, s, 1), F32)],
        compiler_params=_params(("parallel", "arbitrary")),
    )(qkv, qkv, qkv)


def _attn_bwd(name, qkv, lam, do, nh):
    s, d3 = qkv.shape
    d = d3 // 3
    dh = d // nh
    tq, kb = _attn_tiles(s)
    per = tq // kb
    scale = 1.0 / math.sqrt(dh)

    def body(q_ref, k_ref, v_ref, do_ref, lam_ref, dq_ref, dk_ref, dv_ref):
        i = pl.program_id(1)

        @pl.when(i == 0)
        def _():
            dk_ref[...] = jnp.zeros_like(dk_ref)
            dv_ref[...] = jnp.zeros_like(dv_ref)
        q = q_ref[...]
        dout = do_ref[...]
        total = lam_ref[...]
        row = lax.broadcasted_iota(jnp.int32, (kb, kb), 0)
        col = lax.broadcasted_iota(jnp.int32, (kb, kb), 1)
        upto = (row <= col).astype(BF)
        before = (row < col).astype(BF)
        qpos = i * tq + lax.broadcasted_iota(jnp.int32, (tq, kb), 0)
        kcol = lax.broadcasted_iota(jnp.int32, (tq, kb), 1)

        def block(b, carry, masked):
            dq, l_carry, g_carry = carry
            off = pl.multiple_of(b * kb, kb)
            kt = k_ref[pl.ds(off, kb), :]
            vt = v_ref[pl.ds(off, kb), :]
            z = lax.dot_general(q, kt, NT_DIMS, preferred_element_type=F32) * scale
            lraw = -_softplus(z)
            if masked:
                mask = (off + kcol) < qpos
                lm = jnp.where(mask, lraw, 0.0)
            else:
                lm = lraw
            logsig = z + lraw
            w = jnp.exp(logsig + total - (_hilo_dot(lm, upto) + l_carry))
            if masked:
                w = jnp.where(mask, w, 0.0)
            g = w * lax.dot_general(dout, vt, NT_DIMS, preferred_element_type=F32)
            sg = jnp.exp(logsig)
            dz = g * (1.0 - sg) - sg * (_hilo_dot(g, before) + g_carry)
            if masked:
                dz = jnp.where(mask, dz, 0.0)
            dz = (dz * scale).astype(BF)
            dq = dq + jnp.dot(dz, kt, preferred_element_type=F32)
            dk_ref[pl.ds(off, kb), :] += lax.dot_general(dz, q, TN_DIMS, preferred_element_type=F32)
            dv_ref[pl.ds(off, kb), :] += lax.dot_general(w.astype(BF), dout, TN_DIMS, preferred_element_type=F32)
            return (dq, l_carry + jnp.sum(lm, axis=1, keepdims=True), g_carry + jnp.sum(g, axis=1, keepdims=True))

        zero_col = jnp.zeros((tq, 1), F32)
        carry = (jnp.zeros((tq, dh), F32), zero_col, zero_col)

        def group(gi, c):
            for jj in range(per):
                c = block(gi * per + jj, c, False)
            return c

        carry = lax.fori_loop(0, i, group, carry)
        for jj in range(per):
            carry = block(i * per + jj, carry, True)
        dq_ref[...] = carry[0]

    blk = pl.BlockSpec((tq, dh), lambda h, i: (i, h))
    head = pl.BlockSpec((s, dh), lambda h, i: (0, h))
    shp = jax.ShapeDtypeStruct((s, d), F32)
    return pl.pallas_call(
        body, grid=(nh, s // tq), name=name,
        in_specs=[blk, pl.BlockSpec((s, dh), lambda h, i: (0, nh + h)),
                  pl.BlockSpec((s, dh), lambda h, i: (0, 2 * nh + h)), blk,
                  pl.BlockSpec((None, tq, 1), lambda h, i: (h, i, 0))],
        out_specs=[blk, head, head], out_shape=[shp, shp, shp],
        compiler_params=_params(("parallel", "arbitrary")),
    )(qkv, qkv, qkv, do, lam)


def _adamw(name, w, g, m, v):
    shape = w.shape
    c = shape[-1]
    r = int(math.prod(shape[:-1])) if len(shape) > 1 else 1
    w2, g2, m2, v2 = (a.reshape(r, c) for a in (w, g, m, v))
    t = min(r, max(SUBLANES, _pow2_floor((1 << 19) // (4 * c))))
    if r % t:
        t = r
    assert r * c * 4 <= (1 << 22) or t < r

    def body(w_ref, g_ref, m_ref, v_ref, d_ref, nm_ref, nv_ref):
        gg = g_ref[...]
        nm = ADAM_B1 * m_ref[...] + (1.0 - ADAM_B1) * gg
        nv = ADAM_B2 * v_ref[...] + (1.0 - ADAM_B2) * (gg * gg)
        m_hat = nm / (1.0 - ADAM_B1 ** ADAM_STEP)
        v_hat = nv / (1.0 - ADAM_B2 ** ADAM_STEP)
        d_ref[...] = -ADAM_LR * (m_hat / (jnp.sqrt(v_hat) + ADAM_EPS) + ADAM_WD * w_ref[...])
        nm_ref[...] = nm
        nv_ref[...] = nv

    spec = pl.BlockSpec((t, c), lambda i: (i, 0))
    shp = jax.ShapeDtypeStruct((r, c), F32)
    d, nm, nv = pl.pallas_call(
        body, grid=(r // t,), name=name, in_specs=[spec] * 4, out_specs=[spec] * 3, out_shape=[shp] * 3,
        compiler_params=_params(("parallel",)),
    )(w2, g2, m2, v2)
    return d.reshape(shape), nm.reshape(shape), nv.reshape(shape)


def _add_row_tile(m, c):
    t = min(m, max(SUBLANES, _pow2_floor((1 << 18) // c)))
    assert m % t == 0
    return t


def _add_half(name, g, theirs, core, out_dtype):
    nb, _, m, c = g.shape
    t = _add_row_tile(m, c)

    def body(core_ref, a_ref, b_ref, o_ref):
        o_ref[...] = (a_ref[...] + b_ref[...]).astype(o_ref.dtype)

    spec = pl.BlockSpec((None, t, c), lambda k, i, core_ref: (k, i, 0))
    return pl.pallas_call(
        body, name=name, out_shape=jax.ShapeDtypeStruct((nb, m, c), out_dtype),
        grid_spec=pltpu.PrefetchScalarGridSpec(
            num_scalar_prefetch=1, grid=(nb, m // t),
            in_specs=[pl.BlockSpec((None, None, t, c), lambda k, i, core_ref: (k, core_ref[0], i, 0)), spec],
            out_specs=spec),
        compiler_params=_params(("parallel", "parallel")))(core, g, theirs)


def _sum_chips(name, a):
    nb, m, c = a.shape
    t = _add_row_tile(m, c)

    def body(a_ref, o_ref):
        acc = a_ref[0].astype(F32)
        for q in range(1, nb):
            acc = acc + a_ref[q].astype(F32)
        o_ref[...] = acc

    return pl.pallas_call(body, grid=(m // t,), name=name,
                          in_specs=[pl.BlockSpec((nb, t, c), lambda i: (0, i, 0))],
                          out_specs=pl.BlockSpec((t, c), lambda i: (i, 0)),
                          out_shape=jax.ShapeDtypeStruct((m, c), F32),
                          compiler_params=_params(("parallel",)))(a)


HBM_SPEC = pl.BlockSpec(memory_space=pltpu.HBM)


def _me():
    return lax.axis_index("x"), lax.axis_index("y"), lax.axis_index("c")


def _remote(src, dst, send_sem, recv_sem, dev):
    return pltpu.make_async_remote_copy(src_ref=src, dst_ref=dst, send_sem=send_sem, recv_sem=recv_sem,
                                        device_id=dev, device_id_type=MESH)


def _other_chips(x, y):
    return [(px, py, 2 * px + py) for px, py in ((1 - x, y), (x, 1 - y), (1 - x, 1 - y))]


def _all_gather_shards(name, shards):
    nw = len(shards)
    halves = [sh.shape[0] // 2 for sh in shards]

    def body(*refs):
        x_refs, out_refs = refs[:nw], refs[nw:2 * nw]
        send_sems, recv_sems, local_sems = refs[2 * nw:]
        x, y, c = _me()
        me, sibling = (x, y, c), (x, y, 1 - c)
        chips = _other_chips(x, y)
        mychip = 2 * x + y

        def blk(i, q, pc):
            return out_refs[i].at[q, pl.ds(pc * halves[i], halves[i]), :]

        def half(i):
            return x_refs[i].at[pl.ds(c * halves[i], halves[i]), :]

        local = [pltpu.make_async_copy(x_refs[i], out_refs[i].at[mychip], local_sems.at[i]) for i in range(nw)]
        for cp in local:
            cp.start()
        first = [_remote(half(i), blk(i, mychip, c), send_sems.at[6 * i + k], recv_sems.at[6 * i + k], (px, py, c))
                 for i in range(nw) for k, (px, py, q) in enumerate(chips)]
        for cp in first:
            cp.start()
        passed = []
        for i in range(nw):
            for k, (px, py, q) in enumerate(chips):
                _remote(half(i), blk(i, q, c), send_sems.at[6 * i + k], recv_sems.at[6 * i + k], me).wait_recv()
                fwd = _remote(blk(i, q, c), blk(i, q, c), send_sems.at[6 * i + 3 + k], recv_sems.at[6 * i + 3 + k],
                              sibling)
                fwd.start()
                passed.append(fwd)
        for i in range(nw):
            for k, (px, py, q) in enumerate(chips):
                _remote(half(i), blk(i, q, 1 - c), send_sems.at[6 * i + 3 + k], recv_sems.at[6 * i + 3 + k],
                        me).wait_recv()
        for cp in first + passed:
            cp.wait_send()
        for cp in local:
            cp.wait()

    return pl.pallas_call(
        body, name=name,
        out_shape=[jax.ShapeDtypeStruct((N_CHIPS,) + tuple(sh.shape), sh.dtype) for sh in shards],
        in_specs=[HBM_SPEC] * nw, out_specs=[HBM_SPEC] * nw,
        scratch_shapes=[pltpu.SemaphoreType.DMA((6 * nw,)), pltpu.SemaphoreType.DMA((6 * nw,)),
                        pltpu.SemaphoreType.DMA((nw,))],
    )(*shards)


def _pair_swap(name, srcs):
    nw = len(srcs)

    def body(*refs):
        src_refs, out_refs = refs[:nw], refs[nw:2 * nw]
        send_sems, recv_sems = refs[2 * nw:]
        x, y, c = _me()
        cps = [_remote(src_refs[i].at[k, 1 - c], out_refs[i].at[k], send_sems.at[N_CHIPS * i + k],
                       recv_sems.at[N_CHIPS * i + k], (x, y, 1 - c))
               for i in range(nw) for k in range(N_CHIPS)]
        for cp in cps:
            cp.start()
        for cp in cps:
            cp.wait()

    return pl.pallas_call(
        body, name=name,
        out_shape=[jax.ShapeDtypeStruct((N_CHIPS,) + tuple(s.shape[2:]), s.dtype) for s in srcs],
        in_specs=[HBM_SPEC] * nw, out_specs=[HBM_SPEC] * nw,
        scratch_shapes=[pltpu.SemaphoreType.DMA((N_CHIPS * nw,)), pltpu.SemaphoreType.DMA((N_CHIPS * nw,))],
    )(*srcs)


def _chip_all_to_all(name, ps):
    nw = len(ps)

    def body(*refs):
        p_refs, out_refs = refs[:nw], refs[nw:2 * nw]
        send_sems, recv_sems, local_sems = refs[2 * nw:]
        x, y, c = _me()
        chips = _other_chips(x, y)
        mychip = 2 * x + y
        local = [pltpu.make_async_copy(p_refs[i].at[mychip], out_refs[i].at[mychip], local_sems.at[i])
                 for i in range(nw)]
        for cp in local:
            cp.start()
        cps = [_remote(p_refs[i].at[q], out_refs[i].at[mychip], send_sems.at[3 * i + k], recv_sems.at[3 * i + k],
                       (px, py, c))
               for i in range(nw) for k, (px, py, q) in enumerate(chips)]
        for cp in cps:
            cp.start()
        for i in range(nw):
            for k, (px, py, q) in enumerate(chips):
                _remote(p_refs[i].at[q], out_refs[i].at[q], send_sems.at[3 * i + k], recv_sems.at[3 * i + k],
                        (px, py, c)).wait_recv()
        for cp in cps:
            cp.wait_send()
        for cp in local:
            cp.wait()

    return pl.pallas_call(
        body, name=name, out_shape=[jax.ShapeDtypeStruct(tuple(p.shape), p.dtype) for p in ps],
        in_specs=[HBM_SPEC] * nw, out_specs=[HBM_SPEC] * nw,
        scratch_shapes=[pltpu.SemaphoreType.DMA((3 * nw,)), pltpu.SemaphoreType.DMA((3 * nw,)),
                        pltpu.SemaphoreType.DMA((nw,))],
    )(*ps)


def _pair_gather(name, hs):
    nw = len(hs)

    def body(*refs):
        h_refs, out_refs = refs[:nw], refs[nw:2 * nw]
        send_sems, recv_sems, local_sems = refs[2 * nw:]
        x, y, c = _me()
        sibling = (x, y, 1 - c)
        local = [pltpu.make_async_copy(h_refs[i], out_refs[i].at[c], local_sems.at[i]) for i in range(nw)]
        for cp in local:
            cp.start()
        cps = [_remote(h_refs[i], out_refs[i].at[c], send_sems.at[i], recv_sems.at[i], sibling) for i in range(nw)]
        for cp in cps:
            cp.start()
        for i in range(nw):
            _remote(h_refs[i], out_refs[i].at[1 - c], send_sems.at[i], recv_sems.at[i], sibling).wait_recv()
        for cp in cps:
            cp.wait_send()
        for cp in local:
            cp.wait()

    return pl.pallas_call(
        body, name=name, out_shape=[jax.ShapeDtypeStruct((2,) + tuple(h.shape), h.dtype) for h in hs],
        in_specs=[HBM_SPEC] * nw, out_specs=[HBM_SPEC] * nw,
        scratch_shapes=[pltpu.SemaphoreType.DMA((nw,)), pltpu.SemaphoreType.DMA((nw,)),
                        pltpu.SemaphoreType.DMA((nw,))],
    )(*hs)


def _small_allreduce(name, buf):
    r, n = buf.shape

    def body(x_ref, o_ref, land_ref, send_sems, recv_sems):
        x, y, c = _me()
        me = 4 * x + 2 * y + c
        land_ref[me] = x_ref[...]
        peers = []
        for k in range(1, N_DEV):
            px = 1 - x if k & 4 else x
            py = 1 - y if k & 2 else y
            pc = 1 - c if k & 1 else c
            peers.append((px, py, pc))
        cps = [_remote(x_ref, land_ref.at[me], send_sems.at[k], recv_sems.at[k], peer)
               for k, peer in enumerate(peers)]
        for cp in cps:
            cp.start()
        for k, (px, py, pc) in enumerate(peers):
            _remote(x_ref, land_ref.at[4 * px + 2 * py + pc], send_sems.at[k], recv_sems.at[k],
                    (px, py, pc)).wait_recv()
        for cp in cps:
            cp.wait_send()
        acc = land_ref[0]
        for q in range(1, N_DEV):
            acc = acc + land_ref[q]
        o_ref[...] = acc

    vmem = pl.BlockSpec(memory_space=pltpu.VMEM)
    return pl.pallas_call(
        body, name=name, out_shape=jax.ShapeDtypeStruct((r, n), F32), in_specs=[vmem], out_specs=vmem,
        scratch_shapes=[pltpu.VMEM((N_DEV, r, n), F32), pltpu.SemaphoreType.DMA((N_DEV - 1,)),
                        pltpu.SemaphoreType.DMA((N_DEV - 1,))],
        compiler_params=pltpu.CompilerParams(vmem_limit_bytes=VMEM_LIMIT_BYTES),
    )(buf)


def _small_layout(shapes):
    offs, off = {}, 0
    for name in SMALL:
        n = int(math.prod(shapes[name]))
        offs[name] = (off, n)
        off += n
    rows = -(-off // (SUBLANES * LANES)) * SUBLANES
    return offs, rows


def _pack_small(vals, offs, rows):
    parts = []
    for name in SMALL:
        off, n = offs[name]
        parts.append(vals[name].reshape(n).astype(F32) if name in vals else jnp.zeros((n,), F32))
    used = sum(p.shape[0] for p in parts)
    parts.append(jnp.zeros((rows * LANES - used,), F32))
    return jnp.concatenate(parts).reshape(rows, LANES)


def _unpack_small(buf, offs, shapes):
    flat = buf.reshape(-1)
    return {name: flat[offs[name][0]:offs[name][0] + offs[name][1]].reshape(shapes[name]) for name in SMALL}


def kernel(x, p, norm_mix_g, norm_mlp_g, norm_ple_g, norm_f_g, w_in_rec, conv_a_w, conv_a_b, ln_a_g, ln_a_b, conv_b_w, conv_b_b, w_rg_a, b_rg_a, w_rg_x, b_rg_x, rg_lambda, w_out_rec, w_qkv, w_o_attn, w_mlp_up, w_mlp_down, w_ple_proj, w_ple_gate, loss_target, m_norm_mix_g, m_norm_mlp_g, m_norm_ple_g, m_norm_f_g, m_w_in_rec, m_conv_a_w, m_conv_a_b, m_ln_a_g, m_ln_a_b, m_conv_b_w, m_conv_b_b, m_w_rg_a, m_b_rg_a, m_w_rg_x, m_b_rg_x, m_rg_lambda, m_w_out_rec, m_w_qkv, m_w_o_attn, m_w_mlp_up, m_w_mlp_down, m_w_ple_proj, m_w_ple_gate, v_norm_mix_g, v_norm_mlp_g, v_norm_ple_g, v_norm_f_g, v_w_in_rec, v_conv_a_w, v_conv_a_b, v_ln_a_g, v_ln_a_b, v_conv_b_w, v_conv_b_b, v_w_rg_a, v_b_rg_a, v_w_rg_x, v_b_rg_x, v_rg_lambda, v_w_out_rec, v_w_qkv, v_w_o_attn, v_w_mlp_up, v_w_mlp_down, v_w_ple_proj, v_w_ple_gate):
    wts = dict(norm_mix_g=norm_mix_g, norm_mlp_g=norm_mlp_g, norm_ple_g=norm_ple_g, norm_f_g=norm_f_g,
               w_in_rec=w_in_rec, conv_a_w=conv_a_w, conv_a_b=conv_a_b, ln_a_g=ln_a_g, ln_a_b=ln_a_b,
               conv_b_w=conv_b_w, conv_b_b=conv_b_b, w_rg_a=w_rg_a, b_rg_a=b_rg_a, w_rg_x=w_rg_x, b_rg_x=b_rg_x,
               rg_lambda=rg_lambda, w_out_rec=w_out_rec, w_qkv=w_qkv, w_o_attn=w_o_attn, w_mlp_up=w_mlp_up,
               w_mlp_down=w_mlp_down, w_ple_proj=w_ple_proj, w_ple_gate=w_ple_gate)
    mom = dict(norm_mix_g=m_norm_mix_g, norm_mlp_g=m_norm_mlp_g, norm_ple_g=m_norm_ple_g, norm_f_g=m_norm_f_g,
               w_in_rec=m_w_in_rec, conv_a_w=m_conv_a_w, conv_a_b=m_conv_a_b, ln_a_g=m_ln_a_g, ln_a_b=m_ln_a_b,
               conv_b_w=m_conv_b_w, conv_b_b=m_conv_b_b, w_rg_a=m_w_rg_a, b_rg_a=m_b_rg_a, w_rg_x=m_w_rg_x,
               b_rg_x=m_b_rg_x, rg_lambda=m_rg_lambda, w_out_rec=m_w_out_rec, w_qkv=m_w_qkv, w_o_attn=m_w_o_attn,
               w_mlp_up=m_w_mlp_up, w_mlp_down=m_w_mlp_down, w_ple_proj=m_w_ple_proj, w_ple_gate=m_w_ple_gate)
    var = dict(norm_mix_g=v_norm_mix_g, norm_mlp_g=v_norm_mlp_g, norm_ple_g=v_norm_ple_g, norm_f_g=v_norm_f_g,
               w_in_rec=v_w_in_rec, conv_a_w=v_conv_a_w, conv_a_b=v_conv_a_b, ln_a_g=v_ln_a_g, ln_a_b=v_ln_a_b,
               conv_b_w=v_conv_b_w, conv_b_b=v_conv_b_b, w_rg_a=v_w_rg_a, b_rg_a=v_b_rg_a, w_rg_x=v_w_rg_x,
               b_rg_x=v_b_rg_x, rg_lambda=v_rg_lambda, w_out_rec=v_w_out_rec, w_qkv=v_w_qkv, w_o_attn=v_w_o_attn,
               w_mlp_up=v_w_mlp_up, w_mlp_down=v_w_mlp_down, w_ple_proj=v_w_ple_proj, w_ple_gate=v_w_ple_gate)

    xi, yi, ci = lax.axis_index("x"), lax.axis_index("y"), lax.axis_index("c")
    chip = 2 * xi + yi
    seq, dm = x.shape[1], x.shape[2]
    depth = norm_mix_g.shape[0]
    dh2 = dm // 2
    hd = dh2 // RG_HEADS
    kw_a, kw_b = conv_a_w.shape[1], conv_b_w.shape[1]
    cshard = conv_a_w.shape[2]

    small_shapes = {n: tuple(wts[n].shape) for n in SMALL}
    small_shapes["conv_a_w"] = (conv_a_w.shape[0], kw_a, dh2)
    small_shapes["conv_b_w"] = (conv_b_w.shape[0], kw_b, dh2)
    offs, small_rows = _small_layout(small_shapes)
    south = (ci == 0).astype(F32)
    placed = {}
    for n in ("conv_a_w", "conv_b_w"):
        placed[n] = lax.dynamic_update_slice(jnp.zeros(small_shapes[n], F32), wts[n] * south,
                                             (0, 0, chip * cshard))
    conv_full = _unpack_small(_small_allreduce("small_ar_conv", _pack_small(placed, offs, small_rows)),
                              offs, small_shapes)
    conv_a_full, conv_b_full = conv_full["conv_a_w"], conv_full["conv_b_w"]

    def layer_mats(i):
        j = i // 2
        mix = ["w_in_rec", "w_out_rec"] if i % 2 == 0 else ["w_qkv", "w_o_attn"]
        names = mix + ["w_mlp_up", "w_mlp_down", "w_ple_proj", "w_ple_gate"]
        return [(n, j if n in mix else i) for n in names]

    core = ci.astype(jnp.int32).reshape(1)

    def gather_layer(i):
        mats = layer_mats(i)
        fulls = _all_gather_shards(f"ag_l{i}", [wts[n][idx].astype(BF) for n, idx in mats])
        out = {}
        for (n, idx), full in zip(mats, fulls):
            kk, nn = wts[n].shape[1:]
            out[n] = full if n in COL_SHARDED else full.reshape(1, N_CHIPS * kk, nn)
        return out

    def reduce_layer(i, grads):
        mats = layer_mats(i)
        parts = []
        for n, _ in mats:
            kk, nn = wts[n].shape[1:]
            parts.append(grads[n].reshape(N_CHIPS, 2, kk // 2, nn))
        theirs = _pair_swap(f"rs_pair_l{i}", parts)
        chip_sums = [_add_half(f"rs_add_{n}_l{i}", g, t, core, BF) for (n, _), g, t in zip(mats, parts, theirs)]
        landed = _chip_all_to_all(f"rs_a2a_l{i}", chip_sums)
        halves = [_sum_chips(f"rs_sum_{n}_l{i}", a) for (n, _), a in zip(mats, landed)]
        both = _pair_gather(f"rs_gather_l{i}", halves)
        return {n: b.reshape(wts[n].shape[1:]) for (n, _), b in zip(mats, both)}

    row2 = lambda a: a.reshape(1, -1)
    to3 = lambda a: a.reshape(seq, RG_HEADS, hd)
    to2 = lambda a: a.reshape(seq, dh2)

    h = x[0]
    saved = []
    for i in range(depth):
        j = i // 2
        w = gather_layer(i)
        sv = dict(w=w, h=h)
        (hn,) = _rows_fwd(f"norm_mix_l{i}", f_norm, [h], [row2(norm_mix_g[i])], [((dm,), BF)])
        sv["hn"] = hn
        if i % 2 == 0:
            u = _mm_nn(f"in_rec_l{i}", hn, w["w_in_rec"], F32)
            (v,) = _rows_fwd(f"glu_l{i}", f_glu, [Cols(u, 0, dh2), Cols(u, 1, dh2)], [], [((dh2,), F32)])
            v3 = to3(v)
            yc = to2(_conv3(f"conv_a_l{i}", v3, conv_a_full[j].reshape(kw_a, RG_HEADS, hd),
                            conv_a_b[j].reshape(RG_HEADS, hd), True))
            (ya,) = _rows_fwd(f"ln_silu_l{i}", f_ln_silu, [yc], [row2(ln_a_g[j]), row2(ln_a_b[j])], [((dh2,), BF)])
            xr3 = to3(u[:, 2 * dh2:3 * dh2])
            xc = to2(_conv3(f"conv_b_l{i}", xr3, conv_b_full[j].reshape(kw_b, RG_HEADS, hd),
                            conv_b_b[j].reshape(RG_HEADS, hd), True))
            ra = _bd_nn(f"rg_a_l{i}", xc, w_rg_a[j])
            ix = _bd_nn(f"rg_x_l{i}", xc, w_rg_x[j])
            gate_params = [row2(b_rg_a[j]), row2(b_rg_x[j]), row2(rg_lambda[j])]
            a, uu = _rows_fwd(f"gates_l{i}", f_gates, [ra, ix, xc], gate_params, [((dh2,), F32), ((dh2,), F32)])
            hs3 = _scan_fwd(f"scan_l{i}", to3(a), to3(uu))
            hs = to2(hs3)
            (yb,) = _rows_fwd(f"gelu_gate_l{i}", f_gelu_gate, [hs, Cols(u, 3, dh2)], [], [((dh2,), BF)])
            cat = jnp.concatenate([ya, yb], axis=1)
            mix = _mm_nn(f"out_rec_l{i}", cat, w["w_out_rec"], F32)
            sv.update(u=u, v3=v3, yc=yc, xr3=xr3, xc=xc, ra=ra, ix=ix, a=a, hs3=hs3, hs=hs, cat=cat,
                      gate_params=gate_params)
        else:
            qkv = _mm_nn(f"qkv_l{i}", hn, w["w_qkv"], BF)
            o, lam = _attn_fwd(f"attn_l{i}", qkv, SB_HEADS)
            mix = _mm_nn(f"o_attn_l{i}", o, w["w_o_attn"], F32)
            sv.update(qkv=qkv, o=o, lam=lam)
        h1, hn2 = _rows_fwd(f"norm_mlp_l{i}", f_add_norm, [h, mix], [row2(norm_mlp_g[i])], [((dm,), F32), ((dm,), BF)])
        up = _mm_nn(f"mlp_up_l{i}", hn2, w["w_mlp_up"], F32)
        (act,) = _rows_fwd(f"relu2_l{i}", f_relu2, [up], [], [((up.shape[1],), BF)])
        mlp = _mm_nn(f"mlp_down_l{i}", act, w["w_mlp_down"], F32)
        h2, hn3 = _rows_fwd(f"norm_ple_l{i}", f_add_norm, [h1, mlp], [row2(norm_ple_g[i])], [((dm,), F32), ((dm,), BF)])
        gpre = _mm_nn(f"ple_gate_l{i}", hn3, w["w_ple_gate"], F32)
        pb = p[i, 0].astype(BF)
        pp = _mm_nn(f"ple_proj_l{i}", pb, w["w_ple_proj"], F32)
        (h3,) = _rows_fwd(f"ple_l{i}", f_ple, [h2, pp, gpre], [], [((dm,), F32)])
        sv.update(mix=mix, h1=h1, hn2=hn2, up=up, act=act, mlp=mlp, h2=h2, hn3=hn3, gpre=gpre, pb=pb, pp=pp)
        saved.append(sv)
        h = h3

    loss_vec, dh, g_norm_f = _loss_and_grad("loss_head", h, loss_target[0], row2(norm_f_g))
    loss = lax.psum(loss_vec[0, 0], ("x", "y", "c"))

    big_grads = {n: [None] * wts[n].shape[0] for n in BIG}
    small_grads = {n: [None] * wts[n].shape[0] for n in SMALL if n != "norm_f_g"}
    for i in reversed(range(depth)):
        j = i // 2
        sv = saved[i]
        w = sv["w"]
        lg = {}
        (d_h2, d_pp, d_gpre), _ = _rows_bwd(f"ple_bwd_l{i}", f_ple, [sv["h2"], sv["pp"], sv["gpre"]], [], [dh],
                                            [F32, BF, BF])
        lg["w_ple_proj"] = _mm_tn(f"ple_proj_dw_l{i}", sv["pb"], d_pp, N_CHIPS)
        lg["w_ple_gate"] = _mm_tn(f"ple_gate_dw_l{i}", sv["hn3"], d_gpre, 1)
        d_hn3 = _mm_nt(f"ple_gate_dx_l{i}", d_gpre, w["w_ple_gate"], F32)
        (d_h1, d_mlp), (g_ple,) = _rows_bwd(f"norm_ple_bwd_l{i}", f_add_norm, [sv["h1"], sv["mlp"]],
                                            [row2(norm_ple_g[i])], [d_h2, d_hn3], [F32, BF])
        lg["w_mlp_down"] = _mm_tn(f"mlp_down_dw_l{i}", sv["act"], d_mlp, 1)
        d_act = _mm_nt(f"mlp_down_dx_l{i}", d_mlp, w["w_mlp_down"], F32)
        (d_up,), _ = _rows_bwd(f"relu2_bwd_l{i}", f_relu2, [sv["up"]], [], [d_act], [BF])
        lg["w_mlp_up"] = _mm_tn(f"mlp_up_dw_l{i}", sv["hn2"], d_up, N_CHIPS)
        d_hn2 = _mm_nt(f"mlp_up_dx_l{i}", d_up, w["w_mlp_up"], F32)
        (d_h0, d_mix), (g_mlp,) = _rows_bwd(f"norm_mlp_bwd_l{i}", f_add_norm, [sv["h"], sv["mix"]],
                                            [row2(norm_mlp_g[i])], [d_h1, d_hn2], [F32, BF])
        if i % 2 == 0:
            u = sv["u"]
            lg["w_out_rec"] = _mm_tn(f"out_rec_dw_l{i}", sv["cat"], d_mix, 1)
            d_cat = _mm_nt(f"out_rec_dx_l{i}", d_mix, w["w_out_rec"], F32)
            (d_yc,), (g_ln_g, g_ln_b) = _rows_bwd(f"ln_silu_bwd_l{i}", f_ln_silu, [sv["yc"]],
                                                  [row2(ln_a_g[j]), row2(ln_a_b[j])], [Cols(d_cat, 0, dh2)], [F32])
            d_yc3 = to3(d_yc)
            wa3 = conv_a_full[j].reshape(kw_a, RG_HEADS, hd)
            d_v = to2(_conv3(f"conv_a_dx_l{i}", d_yc3, wa3, None, False))
            g_ca_w, g_ca_b = _conv3_bwd_w(f"conv_a_dw_l{i}", d_yc3, sv["v3"], kw_a)
            (d_aval, d_agate), _ = _rows_bwd(f"glu_bwd_l{i}", f_glu, [Cols(u, 0, dh2), Cols(u, 1, dh2)], [], [d_v],
                                             [BF, BF])
            (d_hs, d_gr), _ = _rows_bwd(f"gelu_gate_bwd_l{i}", f_gelu_gate, [sv["hs"], Cols(u, 3, dh2)], [],
                                        [Cols(d_cat, 1, dh2)], [F32, BF])
            hprev3 = jnp.concatenate([jnp.zeros((1, RG_HEADS, hd), F32), sv["hs3"][:-1]], axis=0)
            da3, du3 = _scan_bwd(f"scan_bwd_l{i}", to3(sv["a"]), to3(d_hs), hprev3)
            (d_ra, d_ix, d_xc0), (g_ba, g_bx, g_lam) = _rows_bwd(
                f"gates_bwd_l{i}", f_gates, [sv["ra"], sv["ix"], sv["xc"]], sv["gate_params"],
                [to2(da3), to2(du3)], [BF, BF, F32])
            g_wa = _bd_tn(f"rg_a_dw_l{i}", sv["xc"], d_ra, RG_HEADS)
            g_wx = _bd_tn(f"rg_x_dw_l{i}", sv["xc"], d_ix, RG_HEADS)
            d_xc = _bd_nt2(f"rg_dx_l{i}", d_xc0, d_ra, w_rg_a[j], d_ix, w_rg_x[j])
            d_xc3 = to3(d_xc)
            wb3 = conv_b_full[j].reshape(kw_b, RG_HEADS, hd)
            d_xr = to2(_conv3(f"conv_b_dx_l{i}", d_xc3, wb3, None, False))
            g_cb_w, g_cb_b = _conv3_bwd_w(f"conv_b_dw_l{i}", d_xc3, sv["xr3"], kw_b)
            d_u = jnp.concatenate([d_aval, d_agate, d_xr.astype(BF), d_gr], axis=1)
            lg["w_in_rec"] = _mm_tn(f"in_rec_dw_l{i}", sv["hn"], d_u, N_CHIPS)
            d_hn = _mm_nt(f"in_rec_dx_l{i}", d_u, w["w_in_rec"], F32)
            small_grads["conv_a_w"][j] = g_ca_w.reshape(kw_a, dh2)
            small_grads["conv_a_b"][j] = g_ca_b.reshape(dh2)
            small_grads["ln_a_g"][j] = g_ln_g.reshape(dh2)
            small_grads["ln_a_b"][j] = g_ln_b.reshape(dh2)
            small_grads["conv_b_w"][j] = g_cb_w.reshape(kw_b, dh2)
            small_grads["conv_b_b"][j] = g_cb_b.reshape(dh2)
            small_grads["w_rg_a"][j] = g_wa
            small_grads["b_rg_a"][j] = g_ba.reshape(dh2)
            small_grads["w_rg_x"][j] = g_wx
            small_grads["b_rg_x"][j] = g_bx.reshape(dh2)
            small_grads["rg_lambda"][j] = g_lam.reshape(dh2)
        else:
            lg["w_o_attn"] = _mm_tn(f"o_attn_dw_l{i}", sv["o"], d_mix, 1)
            d_o = _mm_nt(f"o_attn_dx_l{i}", d_mix, w["w_o_attn"], BF)
            dq, dk, dv = _attn_bwd(f"attn_bwd_l{i}", sv["qkv"], sv["lam"], d_o, SB_HEADS)
            d_qkv = jnp.concatenate([dq, dk, dv], axis=1).astype(BF)
            lg["w_qkv"] = _mm_tn(f"qkv_dw_l{i}", sv["hn"], d_qkv, N_CHIPS)
            d_hn = _mm_nt(f"qkv_dx_l{i}", d_qkv, w["w_qkv"], F32)
        (dh,), (g_mix,) = _rows_bwd(f"norm_mix_bwd_l{i}", f_norm, [sv["h"]], [row2(norm_mix_g[i])], [d_hn], [F32],
                                    addend=d_h0)
        small_grads["norm_mix_g"][i] = g_mix.reshape(dm)
        small_grads["norm_mlp_g"][i] = g_mlp.reshape(dm)
        small_grads["norm_ple_g"][i] = g_ple.reshape(dm)
        reduced = reduce_layer(i, lg)
        for n, idx in layer_mats(i):
            big_grads[n][idx] = reduced[n]
    grad_x = dh[None]

    small_vals = {n: jnp.stack(small_grads[n]) for n in small_grads}
    small_vals["norm_f_g"] = g_norm_f.reshape(dm)
    reduced_small = _unpack_small(_small_allreduce("small_ar_grads", _pack_small(small_vals, offs, small_rows)),
                                  offs, small_shapes)
    grads = {}
    for n in SMALL:
        g = reduced_small[n]
        if n in ("conv_a_w", "conv_b_w"):
            g = lax.dynamic_slice_in_dim(g, chip * cshard, cshard, axis=2)
        grads[n] = g
    for n in BIG:
        grads[n] = jnp.stack(big_grads[n])

    delta, new_m, new_v = {}, {}, {}
    for n in WEIGHTS:
        delta[n], new_m[n], new_v[n] = _adamw(f"adamw_{n}", wts[n], grads[n], mom[n], var[n])
    return (loss, grad_x, *[grads[n] for n in WEIGHTS], *[delta[n] for n in WEIGHTS],
            *[new_m[n] for n in WEIGHTS], *[new_v[n] for n in WEIGHTS])
```

```python
import math

import jax
import jax.numpy as jnp
from jax import lax
from jax.experimental import pallas as pl
from jax.experimental.pallas import tpu as pltpu

F32 = jnp.float32
BF = jnp.bfloat16
MESH = pl.DeviceIdType.MESH

VMEM_LIMIT_BYTES = 56 * 1024 * 1024
LANES = 128
SUBLANES = 8
N_CHIPS = 4
N_DEV = 8

EPS = 1e-6
SB_HEADS = 16
RG_HEADS = 8
RG_C = 8.0
ADAM_LR = 0.001
ADAM_B1 = 0.9
ADAM_B2 = 0.999
ADAM_EPS = 1e-08
ADAM_WD = 0.01
ADAM_STEP = 10

BIG = ("w_in_rec", "w_out_rec", "w_qkv", "w_o_attn", "w_mlp_up", "w_mlp_down", "w_ple_proj", "w_ple_gate")
COL_SHARDED = ("w_in_rec", "w_qkv", "w_mlp_up", "w_ple_proj")
SMALL = ("norm_mix_g", "norm_mlp_g", "norm_ple_g", "norm_f_g", "conv_a_w", "conv_a_b", "ln_a_g", "ln_a_b",
         "conv_b_w", "conv_b_b", "w_rg_a", "b_rg_a", "w_rg_x", "b_rg_x", "rg_lambda")
WEIGHTS = ("norm_mix_g", "norm_mlp_g", "norm_ple_g", "norm_f_g", "w_in_rec", "conv_a_w", "conv_a_b", "ln_a_g",
           "ln_a_b", "conv_b_w", "conv_b_b", "w_rg_a", "b_rg_a", "w_rg_x", "b_rg_x", "rg_lambda", "w_out_rec",
           "w_qkv", "w_o_attn", "w_mlp_up", "w_mlp_down", "w_ple_proj", "w_ple_gate")


def _params(sem):
    return pltpu.CompilerParams(dimension_semantics=sem, vmem_limit_bytes=VMEM_LIMIT_BYTES)


def _pow2_floor(n):
    return 1 << (int(n).bit_length() - 1)


def _sig(x):
    return 0.5 * (jnp.tanh(0.5 * x) + 1.0)


def _softplus(x):
    return jnp.maximum(x, 0.0) + jnp.log(1.0 + jnp.exp(-jnp.maximum(x, -x)))


def _rms(x, g):
    return x * lax.rsqrt(jnp.mean(x * x, axis=-1, keepdims=True) + EPS) * g


def _neg_expm1(x):
    series = -x * (1.0 + 0.5 * x * (1.0 + x * (1.0 / 3.0) * (1.0 + 0.25 * x)))
    return jnp.where(x > -1e-2, series, 1.0 - jnp.exp(x))


def f_norm(h, g):
    return (_rms(h, g),)


def f_add_norm(h, mix, g):
    h1 = h + mix
    return h1, _rms(h1, g)


def f_ple(h, pp, gpre):
    return (h + pp * _sig(gpre),)


def f_glu(a, b):
    return (a * _sig(b),)


def f_ln_silu(x, g, b):
    mu = jnp.mean(x, axis=-1, keepdims=True)
    xc = x - mu
    var = jnp.mean(xc * xc, axis=-1, keepdims=True)
    y = xc * lax.rsqrt(var + EPS) * g + b
    return (y * _sig(y),)


def f_gelu_gate(hs, gr):
    inner = math.sqrt(2.0 / math.pi) * (gr + 0.044715 * gr * gr * gr)
    return (hs * (0.5 * gr * (1.0 + jnp.tanh(inner))),)


def f_gates(ra, ix, xc, b_a, b_x, lam):
    r = _sig(ra + b_a)
    i = _sig(ix + b_x)
    log_a = -RG_C * r * _softplus(-lam)
    a = jnp.exp(log_a)
    mult = jnp.sqrt(_neg_expm1(2.0 * log_a))
    return a, mult * (i * xc)


class Cols:
    def __init__(self, arr, blk, width):
        self.arr, self.blk, self.width = arr, blk, width
        self.shape = (arr.shape[0], width)
        self.dtype = arr.dtype


def _row_spec(a, t):
    if isinstance(a, Cols):
        blk = a.blk
        return pl.BlockSpec((t, a.width), lambda i: (i, blk))
    nd = len(a.shape)
    return pl.BlockSpec((t,) + tuple(a.shape[1:]), lambda i: (i,) + (0,) * (nd - 1))


def _full_spec(a):
    nd = len(a.shape)
    return pl.BlockSpec(tuple(a.shape), lambda i: (0,) * nd)


def _arr(a):
    return a.arr if isinstance(a, Cols) else a


def _row_tile(shapes):
    s = shapes[0][0]
    widest = max(int(math.prod(sh[1:])) for sh in shapes)
    t = _pow2_floor(max(16, (1 << 18) // widest))
    t = min(t, s)
    assert s % t == 0
    return t


def _rows_fwd(name, fn, rows, params, outs):
    s = rows[0].shape[0]
    t = _row_tile([r.shape for r in rows] + [(s,) + tuple(o[0]) for o in outs])
    nin = len(rows) + len(params)

    def body(*refs):
        vals = [r[...].astype(F32) for r in refs[:nin]]
        res = fn(*vals)
        for o_ref, v in zip(refs[nin:], res):
            o_ref[...] = v.astype(o_ref.dtype)

    out_shape = [jax.ShapeDtypeStruct((s,) + tuple(o[0]), o[1]) for o in outs]
    res = pl.pallas_call(
        body, grid=(s // t,), name=name,
        in_specs=[_row_spec(r, t) for r in rows] + [_full_spec(p) for p in params],
        out_specs=[_row_spec(o, t) for o in out_shape], out_shape=out_shape,
        compiler_params=_params(("parallel",)),
    )(*[_arr(r) for r in rows], *params)
    return res


def _rows_bwd(name, fn, rows, params, cots, out_dtypes, addend=None):
    s = rows[0].shape[0]
    nr, npar, nc = len(rows), len(params), len(cots)
    extra = [addend] if addend is not None else []
    t = _row_tile([r.shape for r in rows] + [c.shape for c in cots])

    def body(*refs):
        rs = [r[...].astype(F32) for r in refs[:nr]]
        ps = [r[...].astype(F32) for r in refs[nr:nr + npar]]
        cs = tuple(r[...].astype(F32) for r in refs[nr + npar:nr + npar + nc])
        k = nr + npar + nc
        ad = refs[k][...].astype(F32) if extra else None
        k += len(extra)
        grow = refs[k:k + nr]
        gpar = refs[k + nr:]
        _, vjp = jax.vjp(fn, *rs, *ps)
        g = vjp(cs)
        for j in range(nr):
            val = g[j]
            if j == 0 and ad is not None:
                val = val + ad
            grow[j][...] = val.astype(grow[j].dtype)
        first = pl.program_id(0) == 0
        for j in range(npar):
            @pl.when(first)
            def _(j=j):
                gpar[j][...] = jnp.zeros_like(gpar[j])
            gpar[j][...] += g[nr + j]

    out_shape = ([jax.ShapeDtypeStruct(tuple(r.shape), dt) for r, dt in zip(rows, out_dtypes)]
                 + [jax.ShapeDtypeStruct(tuple(p.shape), F32) for p in params])
    res = pl.pallas_call(
        body, grid=(s // t,), name=name,
        in_specs=([_row_spec(r, t) for r in rows] + [_full_spec(p) for p in params]
                  + [_row_spec(c, t) for c in cots] + [_row_spec(a, t) for a in extra]),
        out_specs=([_row_spec(o, t) for o in out_shape[:nr]] + [_full_spec(o) for o in out_shape[nr:]]),
        out_shape=out_shape,
        compiler_params=_params(("arbitrary",)),
    )(*[_arr(r) for r in rows], *params, *[_arr(c) for c in cots], *extra)
    return list(res[:nr]), list(res[nr:])


def _loss_and_grad(name, h, tgt, g):
    s, d = h.shape
    t = _row_tile([h.shape])

    def body(h_ref, t_ref, g_ref, loss_ref, dh_ref, dg_ref):
        tg = t_ref[...]

        def f(hh, gg):
            e = _rms(hh, gg) - tg
            return 0.5 * jnp.mean(e * e, axis=-1, keepdims=True)

        val, vjp = jax.vjp(f, h_ref[...], g_ref[...])
        dh, dg = vjp(jnp.ones_like(val))
        dh_ref[...] = dh

        @pl.when(pl.program_id(0) == 0)
        def _():
            loss_ref[...] = jnp.zeros_like(loss_ref)
            dg_ref[...] = jnp.zeros_like(dg_ref)
        loss_ref[...] += jnp.broadcast_to(jnp.sum(val, axis=0, keepdims=True), loss_ref.shape)
        dg_ref[...] += dg

    return pl.pallas_call(
        body, grid=(s // t,), name=name,
        in_specs=[_row_spec(h, t), _row_spec(tgt, t), _full_spec(g)],
        out_specs=[pl.BlockSpec((1, LANES), lambda i: (0, 0)), _row_spec(h, t), _full_spec(g)],
        out_shape=[jax.ShapeDtypeStruct((1, LANES), F32), jax.ShapeDtypeStruct((s, d), F32),
                   jax.ShapeDtypeStruct(tuple(g.shape), F32)],
        compiler_params=_params(("arbitrary",)),
    )(h, tgt, g)


def _col_tile(nc):
    for t in (1024, 768, 512):
        if nc >= t and nc % t == 0:
            return t
    return nc


def _mm(name, a, b, grid, in_specs, out_spec, out_shape, dims, acc_shape):
    nsteps = grid[2]

    def body(a_ref, b_ref, o_ref, acc_ref):
        k = pl.program_id(2)

        @pl.when(k == 0)
        def _():
            acc_ref[...] = jnp.zeros_like(acc_ref)
        acc_ref[...] += lax.dot_general(a_ref[...].astype(BF), b_ref[...].astype(BF), dims,
                                        preferred_element_type=F32)

        @pl.when(k == nsteps - 1)
        def _():
            o_ref[...] = acc_ref[...].astype(o_ref.dtype)

    return pl.pallas_call(
        body, grid=grid, name=name, in_specs=in_specs, out_specs=out_spec, out_shape=out_shape,
        scratch_shapes=[pltpu.VMEM(acc_shape, F32)],
        compiler_params=_params(("parallel", "parallel", "arbitrary")),
    )(a, b)


def _mm_nn(name, a, w, out_dtype):
    m, k = a.shape
    g, k2, nc = w.shape
    assert k == k2
    tm, tk, tn = min(m, 1024), min(k, 1024), _col_tile(nc)
    r = nc // tn
    return _mm(name, a, w, (m // tm, g * r, k // tk),
               [pl.BlockSpec((tm, tk), lambda i, j, l: (i, l)),
                pl.BlockSpec((None, tk, tn), lambda i, j, l: (j // r, l, j % r))],
               pl.BlockSpec((tm, tn), lambda i, j, l: (i, j)),
               jax.ShapeDtypeStruct((m, g * nc), out_dtype), (((1,), (0,)), ((), ())), (tm, tn))


def _mm_nt(name, a, w, out_dtype):
    m, n = a.shape
    g, k, nc = w.shape
    assert n == g * nc
    tm, tk, tn = min(m, 1024), min(k, 1024), _col_tile(nc)
    r = nc // tn
    return _mm(name, a, w, (m // tm, k // tk, g * r),
               [pl.BlockSpec((tm, tn), lambda i, j, l: (i, l)),
                pl.BlockSpec((None, tk, tn), lambda i, j, l: (l // r, j, l % r))],
               pl.BlockSpec((tm, tk), lambda i, j, l: (i, j)),
               jax.ShapeDtypeStruct((m, k), out_dtype), (((1,), (1,)), ((), ())), (tm, tk))


def _mm_nn_relu2(name, a, w):
    m, k = a.shape
    g, _, nc = w.shape
    tm, tk, tn = min(m, 1024), min(k, 1024), _col_tile(nc)
    r = nc // tn
    nsteps = k // tk

    def body(a_ref, b_ref, u_ref, act_ref, acc_ref):
        l = pl.program_id(2)

        @pl.when(l == 0)
        def _():
            acc_ref[...] = jnp.zeros_like(acc_ref)
        acc_ref[...] += jnp.dot(a_ref[...], b_ref[...], preferred_element_type=F32)

        @pl.when(l == nsteps - 1)
        def _():
            u = acc_ref[...]
            u_ref[...] = u.astype(u_ref.dtype)
            pos = jnp.maximum(u, 0.0)
            act_ref[...] = (pos * pos).astype(act_ref.dtype)

    out = pl.BlockSpec((tm, tn), lambda i, j, l: (i, j))
    shp = jax.ShapeDtypeStruct((m, g * nc), BF)
    return pl.pallas_call(
        body, grid=(m // tm, g * r, nsteps), name=name,
        in_specs=[pl.BlockSpec((tm, tk), lambda i, j, l: (i, l)),
                  pl.BlockSpec((None, tk, tn), lambda i, j, l: (j // r, l, j % r))],
        out_specs=[out, out], out_shape=[shp, shp], scratch_shapes=[pltpu.VMEM((tm, tn), F32)],
        compiler_params=_params(("parallel", "parallel", "arbitrary")),
    )(a, w)


def _mm_nt_relu2_bwd(name, a, w, u):
    m, n = a.shape
    g, k, nc = w.shape
    tm, tk, tn = min(m, 1024), min(k, 1024), _col_tile(nc)
    r = nc // tn
    nsteps = g * r

    def body(a_ref, b_ref, u_ref, o_ref, acc_ref):
        l = pl.program_id(2)

        @pl.when(l == 0)
        def _():
            acc_ref[...] = jnp.zeros_like(acc_ref)
        acc_ref[...] += lax.dot_general(a_ref[...], b_ref[...], NT_DIMS, preferred_element_type=F32)

        @pl.when(l == nsteps - 1)
        def _():
            o_ref[...] = (acc_ref[...] * (2.0 * jnp.maximum(u_ref[...].astype(F32), 0.0))).astype(o_ref.dtype)

    tile = pl.BlockSpec((tm, tk), lambda i, j, l: (i, j))
    return pl.pallas_call(
        body, grid=(m // tm, k // tk, nsteps), name=name,
        in_specs=[pl.BlockSpec((tm, tn), lambda i, j, l: (i, l)),
                  pl.BlockSpec((None, tk, tn), lambda i, j, l: (l // r, j, l % r)), tile],
        out_specs=tile, out_shape=jax.ShapeDtypeStruct((m, k), BF), scratch_shapes=[pltpu.VMEM((tm, tk), F32)],
        compiler_params=_params(("parallel", "parallel", "arbitrary")),
    )(a, w, u)


def _mm_tn(name, a, b, g):
    m, k = a.shape
    m2, n = b.shape
    assert m == m2 and n % g == 0
    nc = n // g
    tm, tk, tn = min(m, 1024), min(k, 1024), _col_tile(nc)
    r = nc // tn
    return _mm(name, a, b, (k // tk, g * r, m // tm),
               [pl.BlockSpec((tm, tk), lambda i, j, l: (l, i)),
                pl.BlockSpec((tm, tn), lambda i, j, l: (l, j))],
               pl.BlockSpec((None, tk, tn), lambda i, j, l: (j // r, i, j % r)),
               jax.ShapeDtypeStruct((g, k, nc), F32), (((0,), (0,)), ((), ())), (tk, tn))


def _bd_nn(name, x, w):
    s, c = x.shape
    nh, hd, _ = w.shape
    tm = min(s, 1024)

    def body(x_ref, w_ref, o_ref):
        o_ref[...] = jnp.dot(x_ref[...].astype(BF), w_ref[...].astype(BF), preferred_element_type=F32)

    return pl.pallas_call(
        body, grid=(s // tm, nh), name=name,
        in_specs=[pl.BlockSpec((tm, hd), lambda i, h: (i, h)), pl.BlockSpec((None, hd, hd), lambda i, h: (h, 0, 0))],
        out_specs=pl.BlockSpec((tm, hd), lambda i, h: (i, h)),
        out_shape=jax.ShapeDtypeStruct((s, c), F32),
        compiler_params=_params(("parallel", "parallel")),
    )(x, w)


def _bd_nt2(name, add, dy1, w1, dy2, w2):
    s, c = add.shape
    nh, hd, _ = w1.shape
    tm = min(s, 1024)
    nt = (((1,), (1,)), ((), ()))

    def body(a_ref, d1_ref, w1_ref, d2_ref, w2_ref, o_ref):
        o_ref[...] = (a_ref[...]
                      + lax.dot_general(d1_ref[...].astype(BF), w1_ref[...].astype(BF), nt, preferred_element_type=F32)
                      + lax.dot_general(d2_ref[...].astype(BF), w2_ref[...].astype(BF), nt, preferred_element_type=F32))

    row = pl.BlockSpec((tm, hd), lambda i, h: (i, h))
    wsp = pl.BlockSpec((None, hd, hd), lambda i, h: (h, 0, 0))
    return pl.pallas_call(
        body, grid=(s // tm, nh), name=name, in_specs=[row, row, wsp, row, wsp], out_specs=row,
        out_shape=jax.ShapeDtypeStruct((s, c), F32), compiler_params=_params(("parallel", "parallel")),
    )(add, dy1, w1, dy2, w2)


def _bd_tn(name, x, dy, nh):
    s, c = x.shape
    hd = c // nh
    tm = min(s, 1024)
    tn = (((0,), (0,)), ((), ()))

    def body(x_ref, d_ref, o_ref):
        @pl.when(pl.program_id(1) == 0)
        def _():
            o_ref[...] = jnp.zeros_like(o_ref)
        o_ref[...] += lax.dot_general(x_ref[...].astype(BF), d_ref[...].astype(BF), tn, preferred_element_type=F32)

    row = pl.BlockSpec((tm, hd), lambda h, i: (i, h))
    return pl.pallas_call(
        body, grid=(nh, s // tm), name=name, in_specs=[row, row],
        out_specs=pl.BlockSpec((None, hd, hd), lambda h, i: (h, 0, 0)),
        out_shape=jax.ShapeDtypeStruct((nh, hd, hd), F32), compiler_params=_params(("parallel", "arbitrary")),
    )(x, dy)


CONV_HALO = 32
CONV_SUB = 16


def _conv3(name, x3, w3, b2, causal):
    s, sl, hd = x3.shape
    kw = w3.shape[0]
    tc = min(s, 256)
    nchunks = s // tc
    per = tc // CONV_HALO
    nhalo = s // CONV_HALO
    assert kw - 1 <= CONV_HALO and tc % CONV_HALO == 0 and tc % CONV_SUB == 0
    has_bias = b2 is not None

    def body(*refs):
        if has_bias:
            cur_ref, halo_ref, w_ref, b_ref, y_ref, win_ref = refs
        else:
            cur_ref, halo_ref, w_ref, y_ref, win_ref = refs
        i = pl.program_id(0)
        if causal:
            win_ref[0:CONV_HALO] = jnp.where(i > 0, halo_ref[...], 0.0)
            win_ref[CONV_HALO:CONV_HALO + tc] = cur_ref[...]
        else:
            win_ref[0:tc] = cur_ref[...]
            win_ref[tc:tc + CONV_HALO] = jnp.where(i < nchunks - 1, halo_ref[...], 0.0)

        def sub_step(j, carry):
            t0 = pl.multiple_of(j * CONV_SUB, CONV_SUB)
            if has_bias:
                acc = jnp.broadcast_to(b_ref[...], (CONV_SUB, sl, hd))
            else:
                acc = jnp.zeros((CONV_SUB, sl, hd), F32)
            for k in range(kw):
                off = CONV_HALO - (kw - 1) + k if causal else kw - 1 - k
                acc = acc + w_ref[k] * win_ref[pl.ds(t0 + off, CONV_SUB)]
            y_ref[pl.ds(t0, CONV_SUB)] = acc
            return carry

        lax.fori_loop(0, tc // CONV_SUB, sub_step, 0)

    if causal:
        halo_map = lambda i: (jnp.maximum(i * per - 1, 0), 0, 0)
    else:
        halo_map = lambda i: (jnp.minimum((i + 1) * per, nhalo - 1), 0, 0)
    chunk = pl.BlockSpec((tc, sl, hd), lambda i: (i, 0, 0))
    in_specs = [chunk, pl.BlockSpec((CONV_HALO, sl, hd), halo_map), pl.BlockSpec((kw, sl, hd), lambda i: (0, 0, 0))]
    args = [x3, x3, w3]
    if has_bias:
        in_specs.append(pl.BlockSpec((sl, hd), lambda i: (0, 0)))
        args.append(b2)
    return pl.pallas_call(
        body, grid=(nchunks,), name=name, in_specs=in_specs, out_specs=chunk,
        out_shape=jax.ShapeDtypeStruct((s, sl, hd), F32),
        scratch_shapes=[pltpu.VMEM((tc + CONV_HALO, sl, hd), F32)],
        compiler_params=_params(("parallel",)),
    )(*args)


def _conv3_bwd_w(name, dy3, x3, kw):
    s, sl, hd = x3.shape
    tc = min(s, 256)
    per = tc // CONV_HALO

    def body(dy_ref, cur_ref, halo_ref, dw_ref, db_ref, win_ref):
        i = pl.program_id(0)

        @pl.when(i == 0)
        def _():
            dw_ref[...] = jnp.zeros_like(dw_ref)
            db_ref[...] = jnp.zeros_like(db_ref)
        win_ref[0:CONV_HALO] = jnp.where(i > 0, halo_ref[...], 0.0)
        win_ref[CONV_HALO:CONV_HALO + tc] = cur_ref[...]

        def sub_step(j, carry):
            t0 = pl.multiple_of(j * SUBLANES, SUBLANES)
            dy = dy_ref[pl.ds(t0, SUBLANES)]
            new = [carry[k] + jnp.sum(dy * win_ref[pl.ds(t0 + CONV_HALO - (kw - 1) + k, SUBLANES)], axis=0)
                   for k in range(kw)]
            new.append(carry[kw] + jnp.sum(dy, axis=0))
            return tuple(new)

        zero = jnp.zeros((sl, hd), F32)
        res = lax.fori_loop(0, tc // SUBLANES, sub_step, tuple(zero for _ in range(kw + 1)))
        for k in range(kw):
            dw_ref[k] += res[k]
        db_ref[...] += res[kw]

    chunk = pl.BlockSpec((tc, sl, hd), lambda i: (i, 0, 0))
    return pl.pallas_call(
        body, grid=(s // tc,), name=name,
        in_specs=[chunk, chunk, pl.BlockSpec((CONV_HALO, sl, hd), lambda i: (jnp.maximum(i * per - 1, 0), 0, 0))],
        out_specs=[pl.BlockSpec((kw, sl, hd), lambda i: (0, 0, 0)), pl.BlockSpec((sl, hd), lambda i: (0, 0))],
        out_shape=[jax.ShapeDtypeStruct((kw, sl, hd), F32), jax.ShapeDtypeStruct((sl, hd), F32)],
        scratch_shapes=[pltpu.VMEM((tc + CONV_HALO, sl, hd), F32)],
        compiler_params=_params(("arbitrary",)),
    )(dy3, x3, x3)


def _scan_fwd(name, a3, u3):
    s, sl, hd = a3.shape
    tc = min(s, 512)

    def body(a_ref, u_ref, h_ref, carry_ref):
        @pl.when(pl.program_id(0) == 0)
        def _():
            carry_ref[...] = jnp.zeros_like(carry_ref)

        def step(t, h):
            h = a_ref[t] * h + u_ref[t]
            h_ref[t] = h
            return h

        carry_ref[...] = lax.fori_loop(0, tc, step, carry_ref[...], unroll=8)

    chunk = pl.BlockSpec((tc, sl, hd), lambda i: (i, 0, 0))
    return pl.pallas_call(
        body, grid=(s // tc,), name=name, in_specs=[chunk, chunk], out_specs=chunk,
        out_shape=jax.ShapeDtypeStruct((s, sl, hd), F32), scratch_shapes=[pltpu.VMEM((sl, hd), F32)],
        compiler_params=_params(("arbitrary",)),
    )(a3, u3)


def _scan_bwd(name, a3, gh3, hprev3):
    s, sl, hd = a3.shape
    tc = min(s, 512)
    n = s // tc

    def body(a_ref, g_ref, hp_ref, da_ref, du_ref, carry_ref):
        @pl.when(pl.program_id(0) == 0)
        def _():
            carry_ref[...] = jnp.zeros_like(carry_ref)

        def step(j, c):
            t = tc - 1 - j
            lam = g_ref[t] + c
            du_ref[t] = lam
            da_ref[t] = lam * hp_ref[t]
            return a_ref[t] * lam

        carry_ref[...] = lax.fori_loop(0, tc, step, carry_ref[...], unroll=8)

    chunk = pl.BlockSpec((tc, sl, hd), lambda i: (n - 1 - i, 0, 0))
    shp = jax.ShapeDtypeStruct((s, sl, hd), F32)
    return pl.pallas_call(
        body, grid=(n,), name=name, in_specs=[chunk, chunk, chunk], out_specs=[chunk, chunk],
        out_shape=[shp, shp], scratch_shapes=[pltpu.VMEM((sl, hd), F32)],
        compiler_params=_params(("arbitrary",)),
    )(a3, gh3, hprev3)


ATT_Q = 512
ATT_K = 128
NT_DIMS = (((1,), (1,)), ((), ()))
TN_DIMS = (((0,), (0,)), ((), ()))


def _attn_tiles(s):
    tq = min(s, ATT_Q)
    kb = min(tq, ATT_K)
    assert s % tq == 0 and tq % kb == 0
    return tq, kb


def _hilo_dot(x, u):
    hi = x.astype(BF)
    lo = (x - hi.astype(F32)).astype(BF)
    return jnp.dot(hi, u, preferred_element_type=F32) + jnp.dot(lo, u, preferred_element_type=F32)


def _attn_fwd(name, qkv, nh):
    s, d3 = qkv.shape
    d = d3 // 3
    dh = d // nh
    tq, kb = _attn_tiles(s)
    per = tq // kb
    scale = 1.0 / math.sqrt(dh)

    def body(q_ref, k_ref, v_ref, o_ref, lam_ref):
        i = pl.program_id(1)
        q = q_ref[...]
        after = (lax.broadcasted_iota(jnp.int32, (kb, kb), 0)
                 > lax.broadcasted_iota(jnp.int32, (kb, kb), 1)).astype(BF)
        qpos = i * tq + lax.broadcasted_iota(jnp.int32, (tq, kb), 0)
        kcol = lax.broadcasted_iota(jnp.int32, (tq, kb), 1)

        def block(b, carry, masked):
            acc, tail_carry = carry
            off = pl.multiple_of(b * kb, kb)
            kt = k_ref[pl.ds(off, kb), :]
            vt = v_ref[pl.ds(off, kb), :]
            z = lax.dot_general(q, kt, NT_DIMS, preferred_element_type=F32) * scale
            lraw = -_softplus(z)
            if masked:
                mask = (off + kcol) < qpos
                lm = jnp.where(mask, lraw, 0.0)
            else:
                lm = lraw
            w = jnp.exp(z + lraw + _hilo_dot(lm, after) + tail_carry)
            if masked:
                w = jnp.where(mask, w, 0.0)
            acc = acc + jnp.dot(w.astype(BF), vt, preferred_element_type=F32)
            return acc, tail_carry + jnp.sum(lm, axis=1, keepdims=True)

        carry = (jnp.zeros((tq, dh), F32), jnp.zeros((tq, 1), F32))
        for jj in range(per):
            carry = block((i + 1) * per - 1 - jj, carry, True)

        def group(gi, c):
            for jj in range(per):
                c = block((i - gi) * per - 1 - jj, c, False)
            return c

        acc, total = lax.fori_loop(0, i, group, carry)
        o_ref[...] = acc.astype(o_ref.dtype)
        lam_ref[...] = total

    return pl.pallas_call(
        body, grid=(nh, s // tq), name=name,
        in_specs=[pl.BlockSpec((tq, dh), lambda h, i: (i, h)),
                  pl.BlockSpec((s, dh), lambda h, i: (0, nh + h)),
                  pl.BlockSpec((s, dh), lambda h, i: (0, 2 * nh + h))],
        out_specs=[pl.BlockSpec((tq, dh), lambda h, i: (i, h)), pl.BlockSpec((None, tq, 1), lambda h, i: (h, i, 0))],
        out_shape=[jax.ShapeDtypeStruct((s, d), BF), jax.ShapeDtypeStruct((nh, s, 1), F32)],
        compiler_params=_params(("parallel", "arbitrary")),
    )(qkv, qkv, qkv)


def _attn_bwd(name, qkv, lam, do, nh):
    s, d3 = qkv.shape
    d = d3 // 3
    dh = d // nh
    tq, kb = _attn_tiles(s)
    per = tq // kb
    scale = 1.0 / math.sqrt(dh)

    def body(q_ref, k_ref, v_ref, do_ref, lam_ref, dq_ref, dk_ref, dv_ref):
        i = pl.program_id(1)

        @pl.when(i == 0)
        def _():
            dk_ref[...] = jnp.zeros_like(dk_ref)
            dv_ref[...] = jnp.zeros_like(dv_ref)
        q = q_ref[...]
        dout = do_ref[...]
        total = lam_ref[...]
        row = lax.broadcasted_iota(jnp.int32, (kb, kb), 0)
        col = lax.broadcasted_iota(jnp.int32, (kb, kb), 1)
        upto = (row <= col).astype(BF)
        before = (row < col).astype(BF)
        qpos = i * tq + lax.broadcasted_iota(jnp.int32, (tq, kb), 0)
        kcol = lax.broadcasted_iota(jnp.int32, (tq, kb), 1)

        def block(b, carry, masked):
            dq, l_carry, g_carry = carry
            off = pl.multiple_of(b * kb, kb)
            kt = k_ref[pl.ds(off, kb), :]
            vt = v_ref[pl.ds(off, kb), :]
            z = lax.dot_general(q, kt, NT_DIMS, preferred_element_type=F32) * scale
            lraw = -_softplus(z)
            if masked:
                mask = (off + kcol) < qpos
                lm = jnp.where(mask, lraw, 0.0)
            else:
                lm = lraw
            logsig = z + lraw
            w = jnp.exp(logsig + total - (_hilo_dot(lm, upto) + l_carry))
            if masked:
                w = jnp.where(mask, w, 0.0)
            g = w * lax.dot_general(dout, vt, NT_DIMS, preferred_element_type=F32)
            sg = jnp.exp(logsig)
            dz = g * (1.0 - sg) - sg * (_hilo_dot(g, before) + g_carry)
            if masked:
                dz = jnp.where(mask, dz, 0.0)
            dz = (dz * scale).astype(BF)
            dq = dq + jnp.dot(dz, kt, preferred_element_type=F32)
            dk_ref[pl.ds(off, kb), :] += lax.dot_general(dz, q, TN_DIMS, preferred_element_type=F32)
            dv_ref[pl.ds(off, kb), :] += lax.dot_general(w.astype(BF), dout, TN_DIMS, preferred_element_type=F32)
            return (dq, l_carry + jnp.sum(lm, axis=1, keepdims=True), g_carry + jnp.sum(g, axis=1, keepdims=True))

        zero_col = jnp.zeros((tq, 1), F32)
        carry = (jnp.zeros((tq, dh), F32), zero_col, zero_col)

        def group(gi, c):
            for jj in range(per):
                c = block(gi * per + jj, c, False)
            return c

        carry = lax.fori_loop(0, i, group, carry)
        for jj in range(per):
            carry = block(i * per + jj, carry, True)
        dq_ref[...] = carry[0]

    blk = pl.BlockSpec((tq, dh), lambda h, i: (i, h))
    head = pl.BlockSpec((s, dh), lambda h, i: (0, h))
    shp = jax.ShapeDtypeStruct((s, d), F32)
    return pl.pallas_call(
        body, grid=(nh, s // tq), name=name,
        in_specs=[blk, pl.BlockSpec((s, dh), lambda h, i: (0, nh + h)),
                  pl.BlockSpec((s, dh), lambda h, i: (0, 2 * nh + h)), blk,
                  pl.BlockSpec((None, tq, 1), lambda h, i: (h, i, 0))],
        out_specs=[blk, head, head], out_shape=[shp, shp, shp],
        compiler_params=_params(("parallel", "arbitrary")),
    )(qkv, qkv, qkv, do, lam)


def _adamw(name, w, g, m, v):
    shape = w.shape
    c = shape[-1]
    r = int(math.prod(shape[:-1])) if len(shape) > 1 else 1
    w2, g2, m2, v2 = (a.reshape(r, c) for a in (w, g, m, v))
    t = min(r, max(SUBLANES, _pow2_floor((1 << 19) // (4 * c))))
    if r % t:
        t = r
    assert r * c * 4 <= (1 << 22) or t < r

    def body(w_ref, g_ref, m_ref, v_ref, d_ref, nm_ref, nv_ref):
        gg = g_ref[...]
        nm = ADAM_B1 * m_ref[...] + (1.0 - ADAM_B1) * gg
        nv = ADAM_B2 * v_ref[...] + (1.0 - ADAM_B2) * (gg * gg)
        m_hat = nm / (1.0 - ADAM_B1 ** ADAM_STEP)
        v_hat = nv / (1.0 - ADAM_B2 ** ADAM_STEP)
        d_ref[...] = -ADAM_LR * (m_hat / (jnp.sqrt(v_hat) + ADAM_EPS) + ADAM_WD * w_ref[...])
        nm_ref[...] = nm
        nv_ref[...] = nv

    spec = pl.BlockSpec((t, c), lambda i: (i, 0))
    shp = jax.ShapeDtypeStruct((r, c), F32)
    d, nm, nv = pl.pallas_call(
        body, grid=(r // t,), name=name, in_specs=[spec] * 4, out_specs=[spec] * 3, out_shape=[shp] * 3,
        compiler_params=_params(("parallel",)),
    )(w2, g2, m2, v2)
    return d.reshape(shape), nm.reshape(shape), nv.reshape(shape)


def _add_row_tile(m, c):
    t = min(m, max(SUBLANES, _pow2_floor((1 << 18) // c)))
    assert m % t == 0
    return t


def _add_half(name, g, theirs, core, out_dtype):
    nb, _, m, c = g.shape
    t = _add_row_tile(m, c)

    def body(core_ref, a_ref, b_ref, o_ref):
        o_ref[...] = (a_ref[...] + b_ref[...]).astype(o_ref.dtype)

    spec = pl.BlockSpec((None, t, c), lambda k, i, core_ref: (k, i, 0))
    return pl.pallas_call(
        body, name=name, out_shape=jax.ShapeDtypeStruct((nb, m, c), out_dtype),
        grid_spec=pltpu.PrefetchScalarGridSpec(
            num_scalar_prefetch=1, grid=(nb, m // t),
            in_specs=[pl.BlockSpec((None, None, t, c), lambda k, i, core_ref: (k, core_ref[0], i, 0)), spec],
            out_specs=spec),
        compiler_params=_params(("parallel", "parallel")))(core, g, theirs)


def _sum_chips(name, own, landed, chip):
    _, m, c = own.shape
    t = _add_row_tile(m, c)

    def body(chip_ref, own_ref, land_ref, o_ref):
        me = chip_ref[0]
        mine = own_ref[...].astype(F32)
        acc = None
        for q in range(N_CHIPS):
            rel = jnp.bitwise_xor(me, q)
            slot = jnp.where(rel == 2, 0, jnp.where(rel == 1, 1, 2))
            term = jnp.where(rel == 0, mine, land_ref[slot].astype(F32))
            acc = term if acc is None else acc + term
        o_ref[...] = acc

    return pl.pallas_call(
        body, name=name, out_shape=jax.ShapeDtypeStruct((m, c), F32),
        grid_spec=pltpu.PrefetchScalarGridSpec(
            num_scalar_prefetch=1, grid=(m // t,),
            in_specs=[pl.BlockSpec((None, t, c), lambda i, chip_ref: (chip_ref[0], i, 0)),
                      pl.BlockSpec((3, t, c), lambda i, chip_ref: (0, i, 0))],
            out_specs=pl.BlockSpec((t, c), lambda i, chip_ref: (i, 0))),
        compiler_params=_params(("parallel",)))(chip, own, landed)


HBM_SPEC = pl.BlockSpec(memory_space=pltpu.HBM)


def _me():
    return lax.axis_index("x"), lax.axis_index("y"), lax.axis_index("c")


def _remote(src, dst, send_sem, recv_sem, dev):
    return pltpu.make_async_remote_copy(src_ref=src, dst_ref=dst, send_sem=send_sem, recv_sem=recv_sem,
                                        device_id=dev, device_id_type=MESH)


def _other_chips(x, y):
    return [(px, py, 2 * px + py) for px, py in ((1 - x, y), (x, 1 - y), (1 - x, 1 - y))]


def _all_gather_shards(name, shards):
    nw = len(shards)
    halves = [sh.shape[0] // 2 for sh in shards]

    def body(*refs):
        x_refs, out_refs = refs[:nw], refs[nw:2 * nw]
        send_sems, recv_sems, local_sems = refs[2 * nw:]
        x, y, c = _me()
        me, sibling = (x, y, c), (x, y, 1 - c)
        chips = _other_chips(x, y)
        mychip = 2 * x + y

        def blk(i, q, pc):
            return out_refs[i].at[q, pl.ds(pc * halves[i], halves[i]), :]

        def half(i):
            return x_refs[i].at[pl.ds(c * halves[i], halves[i]), :]

        local = [pltpu.make_async_copy(x_refs[i], out_refs[i].at[mychip], local_sems.at[i]) for i in range(nw)]
        for cp in local:
            cp.start()
        first = [_remote(half(i), blk(i, mychip, c), send_sems.at[6 * i + k], recv_sems.at[6 * i + k], (px, py, c))
                 for i in range(nw) for k, (px, py, q) in enumerate(chips)]
        for cp in first:
            cp.start()
        passed = []
        for i in range(nw):
            for k, (px, py, q) in enumerate(chips):
                _remote(half(i), blk(i, q, c), send_sems.at[6 * i + k], recv_sems.at[6 * i + k], me).wait_recv()
                fwd = _remote(blk(i, q, c), blk(i, q, c), send_sems.at[6 * i + 3 + k], recv_sems.at[6 * i + 3 + k],
                              sibling)
                fwd.start()
                passed.append(fwd)
        for i in range(nw):
            for k, (px, py, q) in enumerate(chips):
                _remote(half(i), blk(i, q, 1 - c), send_sems.at[6 * i + 3 + k], recv_sems.at[6 * i + 3 + k],
                        me).wait_recv()
        for cp in first + passed:
            cp.wait_send()
        for cp in local:
            cp.wait()

    return pl.pallas_call(
        body, name=name,
        out_shape=[jax.ShapeDtypeStruct((N_CHIPS,) + tuple(sh.shape), sh.dtype) for sh in shards],
        in_specs=[HBM_SPEC] * nw, out_specs=[HBM_SPEC] * nw,
        scratch_shapes=[pltpu.SemaphoreType.DMA((6 * nw,)), pltpu.SemaphoreType.DMA((6 * nw,)),
                        pltpu.SemaphoreType.DMA((nw,))],
    )(*shards)


def _pair_swap(name, srcs):
    nw = len(srcs)

    def body(*refs):
        src_refs, out_refs = refs[:nw], refs[nw:2 * nw]
        send_sems, recv_sems = refs[2 * nw:]
        x, y, c = _me()
        cps = [_remote(src_refs[i].at[k, 1 - c], out_refs[i].at[k], send_sems.at[N_CHIPS * i + k],
                       recv_sems.at[N_CHIPS * i + k], (x, y, 1 - c))
               for i in range(nw) for k in range(N_CHIPS)]
        for cp in cps:
            cp.start()
        for cp in cps:
            cp.wait()

    return pl.pallas_call(
        body, name=name,
        out_shape=[jax.ShapeDtypeStruct((N_CHIPS,) + tuple(s.shape[2:]), s.dtype) for s in srcs],
        in_specs=[HBM_SPEC] * nw, out_specs=[HBM_SPEC] * nw,
        scratch_shapes=[pltpu.SemaphoreType.DMA((N_CHIPS * nw,)), pltpu.SemaphoreType.DMA((N_CHIPS * nw,))],
    )(*srcs)


def _chip_all_to_all(name, ps):
    nw = len(ps)

    def body(*refs):
        p_refs, out_refs = refs[:nw], refs[nw:2 * nw]
        send_sems, recv_sems = refs[2 * nw:]
        x, y, c = _me()
        chips = _other_chips(x, y)
        cps = [_remote(p_refs[i].at[q], out_refs[i].at[k], send_sems.at[3 * i + k], recv_sems.at[3 * i + k],
                       (px, py, c))
               for i in range(nw) for k, (px, py, q) in enumerate(chips)]
        for cp in cps:
            cp.start()
        for cp in cps:
            cp.wait()

    return pl.pallas_call(
        body, name=name, out_shape=[jax.ShapeDtypeStruct((3,) + tuple(p.shape[1:]), p.dtype) for p in ps],
        in_specs=[HBM_SPEC] * nw, out_specs=[HBM_SPEC] * nw,
        scratch_shapes=[pltpu.SemaphoreType.DMA((3 * nw,)), pltpu.SemaphoreType.DMA((3 * nw,))],
    )(*ps)


def _pair_send(name, hs):
    nw = len(hs)

    def body(*refs):
        h_refs, out_refs = refs[:nw], refs[nw:2 * nw]
        send_sems, recv_sems = refs[2 * nw:]
        x, y, c = _me()
        cps = [_remote(h_refs[i], out_refs[i], send_sems.at[i], recv_sems.at[i], (x, y, 1 - c)) for i in range(nw)]
        for cp in cps:
            cp.start()
        for cp in cps:
            cp.wait()

    return pl.pallas_call(
        body, name=name, out_shape=[jax.ShapeDtypeStruct(tuple(h.shape), h.dtype) for h in hs],
        in_specs=[HBM_SPEC] * nw, out_specs=[HBM_SPEC] * nw,
        scratch_shapes=[pltpu.SemaphoreType.DMA((nw,)), pltpu.SemaphoreType.DMA((nw,))],
    )(*hs)


def _small_allreduce(name, buf):
    r, n = buf.shape

    def body(x_ref, o_ref, land_ref, send_sems, recv_sems):
        x, y, c = _me()
        me = 4 * x + 2 * y + c
        land_ref[me] = x_ref[...]
        peers = []
        for k in range(1, N_DEV):
            px = 1 - x if k & 4 else x
            py = 1 - y if k & 2 else y
            pc = 1 - c if k & 1 else c
            peers.append((px, py, pc))
        cps = [_remote(x_ref, land_ref.at[me], send_sems.at[k], recv_sems.at[k], peer)
               for k, peer in enumerate(peers)]
        for cp in cps:
            cp.start()
        for k, (px, py, pc) in enumerate(peers):
            _remote(x_ref, land_ref.at[4 * px + 2 * py + pc], send_sems.at[k], recv_sems.at[k],
                    (px, py, pc)).wait_recv()
        for cp in cps:
            cp.wait_send()
        acc = land_ref[0]
        for q in range(1, N_DEV):
            acc = acc + land_ref[q]
        o_ref[...] = acc

    vmem = pl.BlockSpec(memory_space=pltpu.VMEM)
    return pl.pallas_call(
        body, name=name, out_shape=jax.ShapeDtypeStruct((r, n), F32), in_specs=[vmem], out_specs=vmem,
        scratch_shapes=[pltpu.VMEM((N_DEV, r, n), F32), pltpu.SemaphoreType.DMA((N_DEV - 1,)),
                        pltpu.SemaphoreType.DMA((N_DEV - 1,))],
        compiler_params=pltpu.CompilerParams(vmem_limit_bytes=VMEM_LIMIT_BYTES),
    )(buf)


def _small_layout(shapes):
    offs, off = {}, 0
    for name in SMALL:
        n = int(math.prod(shapes[name]))
        offs[name] = (off, n)
        off += n
    rows = -(-off // (SUBLANES * LANES)) * SUBLANES
    return offs, rows


def _pack_small(vals, offs, rows):
    parts = []
    for name in SMALL:
        off, n = offs[name]
        parts.append(vals[name].reshape(n).astype(F32) if name in vals else jnp.zeros((n,), F32))
    used = sum(p.shape[0] for p in parts)
    parts.append(jnp.zeros((rows * LANES - used,), F32))
    return jnp.concatenate(parts).reshape(rows, LANES)


def _unpack_small(buf, offs, shapes):
    flat = buf.reshape(-1)
    return {name: flat[offs[name][0]:offs[name][0] + offs[name][1]].reshape(shapes[name]) for name in SMALL}


def kernel(x, p, norm_mix_g, norm_mlp_g, norm_ple_g, norm_f_g, w_in_rec, conv_a_w, conv_a_b, ln_a_g, ln_a_b, conv_b_w, conv_b_b, w_rg_a, b_rg_a, w_rg_x, b_rg_x, rg_lambda, w_out_rec, w_qkv, w_o_attn, w_mlp_up, w_mlp_down, w_ple_proj, w_ple_gate, loss_target, m_norm_mix_g, m_norm_mlp_g, m_norm_ple_g, m_norm_f_g, m_w_in_rec, m_conv_a_w, m_conv_a_b, m_ln_a_g, m_ln_a_b, m_conv_b_w, m_conv_b_b, m_w_rg_a, m_b_rg_a, m_w_rg_x, m_b_rg_x, m_rg_lambda, m_w_out_rec, m_w_qkv, m_w_o_attn, m_w_mlp_up, m_w_mlp_down, m_w_ple_proj, m_w_ple_gate, v_norm_mix_g, v_norm_mlp_g, v_norm_ple_g, v_norm_f_g, v_w_in_rec, v_conv_a_w, v_conv_a_b, v_ln_a_g, v_ln_a_b, v_conv_b_w, v_conv_b_b, v_w_rg_a, v_b_rg_a, v_w_rg_x, v_b_rg_x, v_rg_lambda, v_w_out_rec, v_w_qkv, v_w_o_attn, v_w_mlp_up, v_w_mlp_down, v_w_ple_proj, v_w_ple_gate):
    wts = dict(norm_mix_g=norm_mix_g, norm_mlp_g=norm_mlp_g, norm_ple_g=norm_ple_g, norm_f_g=norm_f_g,
               w_in_rec=w_in_rec, conv_a_w=conv_a_w, conv_a_b=conv_a_b, ln_a_g=ln_a_g, ln_a_b=ln_a_b,
               conv_b_w=conv_b_w, conv_b_b=conv_b_b, w_rg_a=w_rg_a, b_rg_a=b_rg_a, w_rg_x=w_rg_x, b_rg_x=b_rg_x,
               rg_lambda=rg_lambda, w_out_rec=w_out_rec, w_qkv=w_qkv, w_o_attn=w_o_attn, w_mlp_up=w_mlp_up,
               w_mlp_down=w_mlp_down, w_ple_proj=w_ple_proj, w_ple_gate=w_ple_gate)
    mom = dict(norm_mix_g=m_norm_mix_g, norm_mlp_g=m_norm_mlp_g, norm_ple_g=m_norm_ple_g, norm_f_g=m_norm_f_g,
               w_in_rec=m_w_in_rec, conv_a_w=m_conv_a_w, conv_a_b=m_conv_a_b, ln_a_g=m_ln_a_g, ln_a_b=m_ln_a_b,
               conv_b_w=m_conv_b_w, conv_b_b=m_conv_b_b, w_rg_a=m_w_rg_a, b_rg_a=m_b_rg_a, w_rg_x=m_w_rg_x,
               b_rg_x=m_b_rg_x, rg_lambda=m_rg_lambda, w_out_rec=m_w_out_rec, w_qkv=m_w_qkv, w_o_attn=m_w_o_attn,
               w_mlp_up=m_w_mlp_up, w_mlp_down=m_w_mlp_down, w_ple_proj=m_w_ple_proj, w_ple_gate=m_w_ple_gate)
    var = dict(norm_mix_g=v_norm_mix_g, norm_mlp_g=v_norm_mlp_g, norm_ple_g=v_norm_ple_g, norm_f_g=v_norm_f_g,
               w_in_rec=v_w_in_rec, conv_a_w=v_conv_a_w, conv_a_b=v_conv_a_b, ln_a_g=v_ln_a_g, ln_a_b=v_ln_a_b,
               conv_b_w=v_conv_b_w, conv_b_b=v_conv_b_b, w_rg_a=v_w_rg_a, b_rg_a=v_b_rg_a, w_rg_x=v_w_rg_x,
               b_rg_x=v_b_rg_x, rg_lambda=v_rg_lambda, w_out_rec=v_w_out_rec, w_qkv=v_w_qkv, w_o_attn=v_w_o_attn,
               w_mlp_up=v_w_mlp_up, w_mlp_down=v_w_mlp_down, w_ple_proj=v_w_ple_proj, w_ple_gate=v_w_ple_gate)

    xi, yi, ci = lax.axis_index("x"), lax.axis_index("y"), lax.axis_index("c")
    chip = 2 * xi + yi
    seq, dm = x.shape[1], x.shape[2]
    depth = norm_mix_g.shape[0]
    dh2 = dm // 2
    hd = dh2 // RG_HEADS
    kw_a, kw_b = conv_a_w.shape[1], conv_b_w.shape[1]
    cshard = conv_a_w.shape[2]

    small_shapes = {n: tuple(wts[n].shape) for n in SMALL}
    small_shapes["conv_a_w"] = (conv_a_w.shape[0], kw_a, dh2)
    small_shapes["conv_b_w"] = (conv_b_w.shape[0], kw_b, dh2)
    offs, small_rows = _small_layout(small_shapes)
    south = (ci == 0).astype(F32)
    placed = {}
    for n in ("conv_a_w", "conv_b_w"):
        placed[n] = lax.dynamic_update_slice(jnp.zeros(small_shapes[n], F32), wts[n] * south,
                                             (0, 0, chip * cshard))
    conv_full = _unpack_small(_small_allreduce("small_ar_conv", _pack_small(placed, offs, small_rows)),
                              offs, small_shapes)
    conv_a_full, conv_b_full = conv_full["conv_a_w"], conv_full["conv_b_w"]

    def layer_mats(i):
        j = i // 2
        mix = ["w_in_rec", "w_out_rec"] if i % 2 == 0 else ["w_qkv", "w_o_attn"]
        names = mix + ["w_mlp_up", "w_mlp_down", "w_ple_proj", "w_ple_gate"]
        return [(n, j if n in mix else i) for n in names]

    core = ci.astype(jnp.int32).reshape(1)
    chip_id = chip.astype(jnp.int32).reshape(1)
    row_half = jnp.arange(2, dtype=ci.dtype).reshape(2, 1, 1)

    def gather_layer(i):
        mats = layer_mats(i)
        fulls = _all_gather_shards(f"ag_l{i}", [wts[n][idx].astype(BF) for n, idx in mats])
        out = {}
        for (n, idx), full in zip(mats, fulls):
            kk, nn = wts[n].shape[1:]
            out[n] = full if n in COL_SHARDED else full.reshape(1, N_CHIPS * kk, nn)
        return out

    def reduce_layer(i, grads):
        mats = layer_mats(i)
        parts = []
        for n, _ in mats:
            kk, nn = wts[n].shape[1:]
            parts.append(grads[n].reshape(N_CHIPS, 2, kk // 2, nn))
        theirs = _pair_swap(f"rs_pair_l{i}", parts)
        chip_sums = [_add_half(f"rs_add_{n}_l{i}", g, t, core, BF) for (n, _), g, t in zip(mats, parts, theirs)]
        landed = _chip_all_to_all(f"rs_a2a_l{i}", chip_sums)
        mine = [_sum_chips(f"rs_sum_{n}_l{i}", own, a, chip_id) for (n, _), own, a in zip(mats, chip_sums, landed)]
        others = _pair_send(f"rs_gather_l{i}", mine)
        out = {}
        for (n, _), a, b in zip(mats, mine, others):
            both = jnp.where(row_half == ci, a[None], b[None])
            out[n] = both.reshape(wts[n].shape[1:])
        return out

    row2 = lambda a: a.reshape(1, -1)
    to3 = lambda a: a.reshape(seq, RG_HEADS, hd)
    to2 = lambda a: a.reshape(seq, dh2)

    h = x[0]
    saved = []
    for i in range(depth):
        j = i // 2
        w = gather_layer(i)
        sv = dict(w=w, h=h)
        (hn,) = _rows_fwd(f"norm_mix_l{i}", f_norm, [h], [row2(norm_mix_g[i])], [((dm,), BF)])
        sv["hn"] = hn
        if i % 2 == 0:
            u = _mm_nn(f"in_rec_l{i}", hn, w["w_in_rec"], F32)
            (v,) = _rows_fwd(f"glu_l{i}", f_glu, [Cols(u, 0, dh2), Cols(u, 1, dh2)], [], [((dh2,), F32)])
            v3 = to3(v)
            yc = to2(_conv3(f"conv_a_l{i}", v3, conv_a_full[j].reshape(kw_a, RG_HEADS, hd),
                            conv_a_b[j].reshape(RG_HEADS, hd), True))
            (ya,) = _rows_fwd(f"ln_silu_l{i}", f_ln_silu, [yc], [row2(ln_a_g[j]), row2(ln_a_b[j])], [((dh2,), BF)])
            xr3 = to3(u[:, 2 * dh2:3 * dh2])
            xc = to2(_conv3(f"conv_b_l{i}", xr3, conv_b_full[j].reshape(kw_b, RG_HEADS, hd),
                            conv_b_b[j].reshape(RG_HEADS, hd), True))
            ra = _bd_nn(f"rg_a_l{i}", xc, w_rg_a[j])
            ix = _bd_nn(f"rg_x_l{i}", xc, w_rg_x[j])
            gate_params = [row2(b_rg_a[j]), row2(b_rg_x[j]), row2(rg_lambda[j])]
            a, uu = _rows_fwd(f"gates_l{i}", f_gates, [ra, ix, xc], gate_params, [((dh2,), F32), ((dh2,), F32)])
            hs3 = _scan_fwd(f"scan_l{i}", to3(a), to3(uu))
            hs = to2(hs3)
            (yb,) = _rows_fwd(f"gelu_gate_l{i}", f_gelu_gate, [hs, Cols(u, 3, dh2)], [], [((dh2,), BF)])
            cat = jnp.concatenate([ya, yb], axis=1)
            mix = _mm_nn(f"out_rec_l{i}", cat, w["w_out_rec"], F32)
            sv.update(u=u, v3=v3, yc=yc, xr3=xr3, xc=xc, ra=ra, ix=ix, a=a, hs3=hs3, hs=hs, cat=cat,
                      gate_params=gate_params)
        else:
            qkv = _mm_nn(f"qkv_l{i}", hn, w["w_qkv"], BF)
            o, lam = _attn_fwd(f"attn_l{i}", qkv, SB_HEADS)
            mix = _mm_nn(f"o_attn_l{i}", o, w["w_o_attn"], F32)
            sv.update(qkv=qkv, o=o, lam=lam)
        h1, hn2 = _rows_fwd(f"norm_mlp_l{i}", f_add_norm, [h, mix], [row2(norm_mlp_g[i])], [((dm,), F32), ((dm,), BF)])
        up, act = _mm_nn_relu2(f"mlp_up_l{i}", hn2, w["w_mlp_up"])
        mlp = _mm_nn(f"mlp_down_l{i}", act, w["w_mlp_down"], F32)
        h2, hn3 = _rows_fwd(f"norm_ple_l{i}", f_add_norm, [h1, mlp], [row2(norm_ple_g[i])], [((dm,), F32), ((dm,), BF)])
        gpre = _mm_nn(f"ple_gate_l{i}", hn3, w["w_ple_gate"], F32)
        pb = p[i, 0].astype(BF)
        pp = _mm_nn(f"ple_proj_l{i}", pb, w["w_ple_proj"], F32)
        (h3,) = _rows_fwd(f"ple_l{i}", f_ple, [h2, pp, gpre], [], [((dm,), F32)])
        sv.update(mix=mix, h1=h1, hn2=hn2, up=up, act=act, mlp=mlp, h2=h2, hn3=hn3, gpre=gpre, pb=pb, pp=pp)
        saved.append(sv)
        h = h3

    loss_vec, dh, g_norm_f = _loss_and_grad("loss_head", h, loss_target[0], row2(norm_f_g))
    loss = lax.psum(loss_vec[0, 0], ("x", "y", "c"))

    big_grads = {n: [None] * wts[n].shape[0] for n in BIG}
    small_grads = {n: [None] * wts[n].shape[0] for n in SMALL if n != "norm_f_g"}
    for i in reversed(range(depth)):
        j = i // 2
        sv = saved[i]
        w = sv["w"]
        lg = {}
        (d_h2, d_pp, d_gpre), _ = _rows_bwd(f"ple_bwd_l{i}", f_ple, [sv["h2"], sv["pp"], sv["gpre"]], [], [dh],
                                            [F32, BF, BF])
        lg["w_ple_proj"] = _mm_tn(f"ple_proj_dw_l{i}", sv["pb"], d_pp, N_CHIPS)
        lg["w_ple_gate"] = _mm_tn(f"ple_gate_dw_l{i}", sv["hn3"], d_gpre, 1)
        d_hn3 = _mm_nt(f"ple_gate_dx_l{i}", d_gpre, w["w_ple_gate"], F32)
        (d_h1, d_mlp), (g_ple,) = _rows_bwd(f"norm_ple_bwd_l{i}", f_add_norm, [sv["h1"], sv["mlp"]],
                                            [row2(norm_ple_g[i])], [d_h2, d_hn3], [F32, BF])
        lg["w_mlp_down"] = _mm_tn(f"mlp_down_dw_l{i}", sv["act"], d_mlp, 1)
        d_up = _mm_nt_relu2_bwd(f"mlp_down_dx_l{i}", d_mlp, w["w_mlp_down"], sv["up"])
        lg["w_mlp_up"] = _mm_tn(f"mlp_up_dw_l{i}", sv["hn2"], d_up, N_CHIPS)
        d_hn2 = _mm_nt(f"mlp_up_dx_l{i}", d_up, w["w_mlp_up"], F32)
        (d_h0, d_mix), (g_mlp,) = _rows_bwd(f"norm_mlp_bwd_l{i}", f_add_norm, [sv["h"], sv["mix"]],
                                            [row2(norm_mlp_g[i])], [d_h1, d_hn2], [F32, BF])
        if i % 2 == 0:
            u = sv["u"]
            lg["w_out_rec"] = _mm_tn(f"out_rec_dw_l{i}", sv["cat"], d_mix, 1)
            d_cat = _mm_nt(f"out_rec_dx_l{i}", d_mix, w["w_out_rec"], F32)
            (d_yc,), (g_ln_g, g_ln_b) = _rows_bwd(f"ln_silu_bwd_l{i}", f_ln_silu, [sv["yc"]],
                                                  [row2(ln_a_g[j]), row2(ln_a_b[j])], [Cols(d_cat, 0, dh2)], [F32])
            d_yc3 = to3(d_yc)
            wa3 = conv_a_full[j].reshape(kw_a, RG_HEADS, hd)
            d_v = to2(_conv3(f"conv_a_dx_l{i}", d_yc3, wa3, None, False))
            g_ca_w, g_ca_b = _conv3_bwd_w(f"conv_a_dw_l{i}", d_yc3, sv["v3"], kw_a)
            (d_aval, d_agate), _ = _rows_bwd(f"glu_bwd_l{i}", f_glu, [Cols(u, 0, dh2), Cols(u, 1, dh2)], [], [d_v],
                                             [BF, BF])
            (d_hs, d_gr), _ = _rows_bwd(f"gelu_gate_bwd_l{i}", f_gelu_gate, [sv["hs"], Cols(u, 3, dh2)], [],
                                        [Cols(d_cat, 1, dh2)], [F32, BF])
            hprev3 = jnp.concatenate([jnp.zeros((1, RG_HEADS, hd), F32), sv["hs3"][:-1]], axis=0)
            da3, du3 = _scan_bwd(f"scan_bwd_l{i}", to3(sv["a"]), to3(d_hs), hprev3)
            (d_ra, d_ix, d_xc0), (g_ba, g_bx, g_lam) = _rows_bwd(
                f"gates_bwd_l{i}", f_gates, [sv["ra"], sv["ix"], sv["xc"]], sv["gate_params"],
                [to2(da3), to2(du3)], [BF, BF, F32])
            g_wa = _bd_tn(f"rg_a_dw_l{i}", sv["xc"], d_ra, RG_HEADS)
            g_wx = _bd_tn(f"rg_x_dw_l{i}", sv["xc"], d_ix, RG_HEADS)
            d_xc = _bd_nt2(f"rg_dx_l{i}", d_xc0, d_ra, w_rg_a[j], d_ix, w_rg_x[j])
            d_xc3 = to3(d_xc)
            wb3 = conv_b_full[j].reshape(kw_b, RG_HEADS, hd)
            d_xr = to2(_conv3(f"conv_b_dx_l{i}", d_xc3, wb3, None, False))
            g_cb_w, g_cb_b = _conv3_bwd_w(f"conv_b_dw_l{i}", d_xc3, sv["xr3"], kw_b)
            d_u = jnp.concatenate([d_aval, d_agate, d_xr.astype(BF), d_gr], axis=1)
            lg["w_in_rec"] = _mm_tn(f"in_rec_dw_l{i}", sv["hn"], d_u, N_CHIPS)
            d_hn = _mm_nt(f"in_rec_dx_l{i}", d_u, w["w_in_rec"], F32)
            small_grads["conv_a_w"][j] = g_ca_w.reshape(kw_a, dh2)
            small_grads["conv_a_b"][j] = g_ca_b.reshape(dh2)
            small_grads["ln_a_g"][j] = g_ln_g.reshape(dh2)
            small_grads["ln_a_b"][j] = g_ln_b.reshape(dh2)
            small_grads["conv_b_w"][j] = g_cb_w.reshape(kw_b, dh2)
            small_grads["conv_b_b"][j] = g_cb_b.reshape(dh2)
            small_grads["w_rg_a"][j] = g_wa
            small_grads["b_rg_a"][j] = g_ba.reshape(dh2)
            small_grads["w_rg_x"][j] = g_wx
            small_grads["b_rg_x"][j] = g_bx.reshape(dh2)
            small_grads["rg_lambda"][j] = g_lam.reshape(dh2)
        else:
            lg["w_o_attn"] = _mm_tn(f"o_attn_dw_l{i}", sv["o"], d_mix, 1)
            d_o = _mm_nt(f"o_attn_dx_l{i}", d_mix, w["w_o_attn"], BF)
            dq, dk, dv = _attn_bwd(f"attn_bwd_l{i}", sv["qkv"], sv["lam"], d_o, SB_HEADS)
            d_qkv = jnp.concatenate([dq, dk, dv], axis=1).astype(BF)
            lg["w_qkv"] = _mm_tn(f"qkv_dw_l{i}", sv["hn"], d_qkv, N_CHIPS)
            d_hn = _mm_nt(f"qkv_dx_l{i}", d_qkv, w["w_qkv"], F32)
        (dh,), (g_mix,) = _rows_bwd(f"norm_mix_bwd_l{i}", f_norm, [sv["h"]], [row2(norm_mix_g[i])], [d_hn], [F32],
                                    addend=d_h0)
        small_grads["norm_mix_g"][i] = g_mix.reshape(dm)
        small_grads["norm_mlp_g"][i] = g_mlp.reshape(dm)
        small_grads["norm_ple_g"][i] = g_ple.reshape(dm)
        reduced = reduce_layer(i, lg)
        for n, idx in layer_mats(i):
            big_grads[n][idx] = reduced[n]
    grad_x = dh[None]

    small_vals = {n: jnp.stack(small_grads[n]) for n in small_grads}
    small_vals["norm_f_g"] = g_norm_f.reshape(dm)
    reduced_small = _unpack_small(_small_allreduce("small_ar_grads", _pack_small(small_vals, offs, small_rows)),
                                  offs, small_shapes)
    grads = {}
    for n in SMALL:
        g = reduced_small[n]
        if n in ("conv_a_w", "conv_b_w"):
            g = lax.dynamic_slice_in_dim(g, chip * cshard, cshard, axis=2)
        grads[n] = g
    for n in BIG:
        grads[n] = jnp.stack(big_grads[n])

    delta, new_m, new_v = {}, {}, {}
    for n in WEIGHTS:
        delta[n], new_m[n], new_v[n] = _adamw(f"adamw_{n}", wts[n], grads[n], mom[n], var[n])
    return (loss, grad_x, *[grads[n] for n in WEIGHTS], *[delta[n] for n in WEIGHTS],
            *[new_m[n] for n in WEIGHTS], *[new_v[n] for n in WEIGHTS])
```

```python
import math

import jax
import jax.numpy as jnp
from jax import lax
from jax.experimental import pallas as pl
from jax.experimental.pallas import tpu as pltpu

F32 = jnp.float32
BF = jnp.bfloat16
MESH = pl.DeviceIdType.MESH

VMEM_LIMIT_BYTES = 56 * 1024 * 1024
LANES = 128
SUBLANES = 8
N_CHIPS = 4
N_DEV = 8

EPS = 1e-6
SB_HEADS = 16
RG_HEADS = 8
RG_C = 8.0
ADAM_LR = 0.001
ADAM_B1 = 0.9
ADAM_B2 = 0.999
ADAM_EPS = 1e-08
ADAM_WD = 0.01
ADAM_STEP = 10

BIG = ("w_in_rec", "w_out_rec", "w_qkv", "w_o_attn", "w_mlp_up", "w_mlp_down", "w_ple_proj", "w_ple_gate")
COL_SHARDED = ("w_in_rec", "w_qkv", "w_mlp_up", "w_ple_proj")
SMALL = ("norm_mix_g", "norm_mlp_g", "norm_ple_g", "norm_f_g", "conv_a_w", "conv_a_b", "ln_a_g", "ln_a_b",
         "conv_b_w", "conv_b_b", "w_rg_a", "b_rg_a", "w_rg_x", "b_rg_x", "rg_lambda")
WEIGHTS = ("norm_mix_g", "norm_mlp_g", "norm_ple_g", "norm_f_g", "w_in_rec", "conv_a_w", "conv_a_b", "ln_a_g",
           "ln_a_b", "conv_b_w", "conv_b_b", "w_rg_a", "b_rg_a", "w_rg_x", "b_rg_x", "rg_lambda", "w_out_rec",
           "w_qkv", "w_o_attn", "w_mlp_up", "w_mlp_down", "w_ple_proj", "w_ple_gate")


def _params(sem):
    return pltpu.CompilerParams(dimension_semantics=sem, vmem_limit_bytes=VMEM_LIMIT_BYTES)


def _pow2_floor(n):
    return 1 << (int(n).bit_length() - 1)


def _sig(x):
    return 0.5 * (jnp.tanh(0.5 * x) + 1.0)


def _softplus(x):
    return jnp.maximum(x, 0.0) + jnp.log(1.0 + jnp.exp(-jnp.maximum(x, -x)))


def _rms(x, g):
    return x * lax.rsqrt(jnp.mean(x * x, axis=-1, keepdims=True) + EPS) * g


def _neg_expm1(x):
    series = -x * (1.0 + 0.5 * x * (1.0 + x * (1.0 / 3.0) * (1.0 + 0.25 * x)))
    return jnp.where(x > -1e-2, series, 1.0 - jnp.exp(x))


def f_norm(h, g):
    return (_rms(h, g),)


def f_add_norm(h, mix, g):
    h1 = h + mix
    return h1, _rms(h1, g)


def f_ple(h, pp, gpre):
    return (h + pp * _sig(gpre),)


def f_glu(a, b):
    return (a * _sig(b),)


def f_ln_silu(x, g, b):
    mu = jnp.mean(x, axis=-1, keepdims=True)
    xc = x - mu
    var = jnp.mean(xc * xc, axis=-1, keepdims=True)
    y = xc * lax.rsqrt(var + EPS) * g + b
    return (y * _sig(y),)


def f_gelu_gate(hs, gr):
    inner = math.sqrt(2.0 / math.pi) * (gr + 0.044715 * gr * gr * gr)
    return (hs * (0.5 * gr * (1.0 + jnp.tanh(inner))),)


def f_gates(ra, ix, xc, b_a, b_x, lam):
    r = _sig(ra + b_a)
    i = _sig(ix + b_x)
    log_a = -RG_C * r * _softplus(-lam)
    a = jnp.exp(log_a)
    mult = jnp.sqrt(_neg_expm1(2.0 * log_a))
    return a, mult * (i * xc)


class Cols:
    def __init__(self, arr, blk, width):
        self.arr, self.blk, self.width = arr, blk, width
        self.shape = (arr.shape[0], width)
        self.dtype = arr.dtype


def _row_spec(a, t):
    if isinstance(a, Cols):
        blk = a.blk
        return pl.BlockSpec((t, a.width), lambda i: (i, blk))
    nd = len(a.shape)
    return pl.BlockSpec((t,) + tuple(a.shape[1:]), lambda i: (i,) + (0,) * (nd - 1))


def _full_spec(a):
    nd = len(a.shape)
    return pl.BlockSpec(tuple(a.shape), lambda i: (0,) * nd)


def _arr(a):
    return a.arr if isinstance(a, Cols) else a


def _row_tile(shapes):
    s = shapes[0][0]
    widest = max(int(math.prod(sh[1:])) for sh in shapes)
    t = _pow2_floor(max(16, (1 << 18) // widest))
    t = min(t, s)
    assert s % t == 0
    return t


def _rows_fwd(name, fn, rows, params, outs):
    s = rows[0].shape[0]
    t = _row_tile([r.shape for r in rows] + [(s,) + tuple(o[0]) for o in outs])
    nin = len(rows) + len(params)

    def body(*refs):
        vals = [r[...].astype(F32) for r in refs[:nin]]
        res = fn(*vals)
        for o_ref, v in zip(refs[nin:], res):
            o_ref[...] = v.astype(o_ref.dtype)

    out_shape = [jax.ShapeDtypeStruct((s,) + tuple(o[0]), o[1]) for o in outs]
    res = pl.pallas_call(
        body, grid=(s // t,), name=name,
        in_specs=[_row_spec(r, t) for r in rows] + [_full_spec(p) for p in params],
        out_specs=[_row_spec(o, t) for o in out_shape], out_shape=out_shape,
        compiler_params=_params(("parallel",)),
    )(*[_arr(r) for r in rows], *params)
    return res


def _rows_bwd(name, fn, rows, params, cots, out_dtypes, addend=None):
    s = rows[0].shape[0]
    nr, npar, nc = len(rows), len(params), len(cots)
    extra = [addend] if addend is not None else []
    t = _row_tile([r.shape for r in rows] + [c.shape for c in cots])

    def body(*refs):
        rs = [r[...].astype(F32) for r in refs[:nr]]
        ps = [r[...].astype(F32) for r in refs[nr:nr + npar]]
        cs = tuple(r[...].astype(F32) for r in refs[nr + npar:nr + npar + nc])
        k = nr + npar + nc
        ad = refs[k][...].astype(F32) if extra else None
        k += len(extra)
        grow = refs[k:k + nr]
        gpar = refs[k + nr:]
        _, vjp = jax.vjp(fn, *rs, *ps)
        g = vjp(cs)
        for j in range(nr):
            val = g[j]
            if j == 0 and ad is not None:
                val = val + ad
            grow[j][...] = val.astype(grow[j].dtype)
        first = pl.program_id(0) == 0
        for j in range(npar):
            @pl.when(first)
            def _(j=j):
                gpar[j][...] = jnp.zeros_like(gpar[j])
            gpar[j][...] += g[nr + j]

    out_shape = ([jax.ShapeDtypeStruct(tuple(r.shape), dt) for r, dt in zip(rows, out_dtypes)]
                 + [jax.ShapeDtypeStruct(tuple(p.shape), F32) for p in params])
    res = pl.pallas_call(
        body, grid=(s // t,), name=name,
        in_specs=([_row_spec(r, t) for r in rows] + [_full_spec(p) for p in params]
                  + [_row_spec(c, t) for c in cots] + [_row_spec(a, t) for a in extra]),
        out_specs=([_row_spec(o, t) for o in out_shape[:nr]] + [_full_spec(o) for o in out_shape[nr:]]),
        out_shape=out_shape,
        compiler_params=_params(("arbitrary",)),
    )(*[_arr(r) for r in rows], *params, *[_arr(c) for c in cots], *extra)
    return list(res[:nr]), list(res[nr:])


def _loss_and_grad(name, h, tgt, g):
    s, d = h.shape
    t = _row_tile([h.shape])

    def body(h_ref, t_ref, g_ref, loss_ref, dh_ref, dg_ref):
        tg = t_ref[...]

        def f(hh, gg):
            e = _rms(hh, gg) - tg
            return 0.5 * jnp.mean(e * e, axis=-1, keepdims=True)

        val, vjp = jax.vjp(f, h_ref[...], g_ref[...])
        dh, dg = vjp(jnp.ones_like(val))
        dh_ref[...] = dh

        @pl.when(pl.program_id(0) == 0)
        def _():
            loss_ref[...] = jnp.zeros_like(loss_ref)
            dg_ref[...] = jnp.zeros_like(dg_ref)
        loss_ref[...] += jnp.broadcast_to(jnp.sum(val, axis=0, keepdims=True), loss_ref.shape)
        dg_ref[...] += dg

    return pl.pallas_call(
        body, grid=(s // t,), name=name,
        in_specs=[_row_spec(h, t), _row_spec(tgt, t), _full_spec(g)],
        out_specs=[pl.BlockSpec((1, LANES), lambda i: (0, 0)), _row_spec(h, t), _full_spec(g)],
        out_shape=[jax.ShapeDtypeStruct((1, LANES), F32), jax.ShapeDtypeStruct((s, d), F32),
                   jax.ShapeDtypeStruct(tuple(g.shape), F32)],
        compiler_params=_params(("arbitrary",)),
    )(h, tgt, g)


def _col_tile(nc):
    for t in (1024, 768, 512):
        if nc >= t and nc % t == 0:
            return t
    return nc


def _mm(name, a, b, grid, in_specs, out_spec, out_shape, dims, acc_shape):
    nsteps = grid[2]

    def body(a_ref, b_ref, o_ref, acc_ref):
        k = pl.program_id(2)

        @pl.when(k == 0)
        def _():
            acc_ref[...] = jnp.zeros_like(acc_ref)
        acc_ref[...] += lax.dot_general(a_ref[...].astype(BF), b_ref[...].astype(BF), dims,
                                        preferred_element_type=F32)

        @pl.when(k == nsteps - 1)
        def _():
            o_ref[...] = acc_ref[...].astype(o_ref.dtype)

    return pl.pallas_call(
        body, grid=grid, name=name, in_specs=in_specs, out_specs=out_spec, out_shape=out_shape,
        scratch_shapes=[pltpu.VMEM(acc_shape, F32)],
        compiler_params=_params(("parallel", "parallel", "arbitrary")),
    )(a, b)


def _mm_nn(name, a, w, out_dtype):
    m, k = a.shape
    g, k2, nc = w.shape
    assert k == k2
    tm, tk, tn = min(m, 1024), min(k, 1024), _col_tile(nc)
    r = nc // tn
    return _mm(name, a, w, (m // tm, g * r, k // tk),
               [pl.BlockSpec((tm, tk), lambda i, j, l: (i, l)),
                pl.BlockSpec((None, tk, tn), lambda i, j, l: (j // r, l, j % r))],
               pl.BlockSpec((tm, tn), lambda i, j, l: (i, j)),
               jax.ShapeDtypeStruct((m, g * nc), out_dtype), (((1,), (0,)), ((), ())), (tm, tn))


def _mm_nt(name, a, w, out_dtype):
    m, n = a.shape
    g, k, nc = w.shape
    assert n == g * nc
    tm, tk, tn = min(m, 1024), min(k, 1024), _col_tile(nc)
    r = nc // tn
    return _mm(name, a, w, (m // tm, k // tk, g * r),
               [pl.BlockSpec((tm, tn), lambda i, j, l: (i, l)),
                pl.BlockSpec((None, tk, tn), lambda i, j, l: (l // r, j, l % r))],
               pl.BlockSpec((tm, tk), lambda i, j, l: (i, j)),
               jax.ShapeDtypeStruct((m, k), out_dtype), (((1,), (1,)), ((), ())), (tm, tk))


def _mm_nn_relu2(name, a, w):
    m, k = a.shape
    g, _, nc = w.shape
    tm, tk, tn = min(m, 1024), min(k, 1024), _col_tile(nc)
    r = nc // tn
    nsteps = k // tk

    def body(a_ref, b_ref, u_ref, act_ref, acc_ref):
        l = pl.program_id(2)

        @pl.when(l == 0)
        def _():
            acc_ref[...] = jnp.zeros_like(acc_ref)
        acc_ref[...] += jnp.dot(a_ref[...], b_ref[...], preferred_element_type=F32)

        @pl.when(l == nsteps - 1)
        def _():
            u = acc_ref[...]
            u_ref[...] = u.astype(u_ref.dtype)
            pos = jnp.maximum(u, 0.0)
            act_ref[...] = (pos * pos).astype(act_ref.dtype)

    out = pl.BlockSpec((tm, tn), lambda i, j, l: (i, j))
    shp = jax.ShapeDtypeStruct((m, g * nc), BF)
    return pl.pallas_call(
        body, grid=(m // tm, g * r, nsteps), name=name,
        in_specs=[pl.BlockSpec((tm, tk), lambda i, j, l: (i, l)),
                  pl.BlockSpec((None, tk, tn), lambda i, j, l: (j // r, l, j % r))],
        out_specs=[out, out], out_shape=[shp, shp], scratch_shapes=[pltpu.VMEM((tm, tn), F32)],
        compiler_params=_params(("parallel", "parallel", "arbitrary")),
    )(a, w)


def _mm_nt_relu2_bwd(name, a, w, u):
    m, n = a.shape
    g, k, nc = w.shape
    tm, tk, tn = min(m, 1024), min(k, 1024), _col_tile(nc)
    r = nc // tn
    nsteps = g * r

    def body(a_ref, b_ref, u_ref, o_ref, acc_ref):
        l = pl.program_id(2)

        @pl.when(l == 0)
        def _():
            acc_ref[...] = jnp.zeros_like(acc_ref)
        acc_ref[...] += lax.dot_general(a_ref[...], b_ref[...], NT_DIMS, preferred_element_type=F32)

        @pl.when(l == nsteps - 1)
        def _():
            o_ref[...] = (acc_ref[...] * (2.0 * jnp.maximum(u_ref[...].astype(F32), 0.0))).astype(o_ref.dtype)

    tile = pl.BlockSpec((tm, tk), lambda i, j, l: (i, j))
    return pl.pallas_call(
        body, grid=(m // tm, k // tk, nsteps), name=name,
        in_specs=[pl.BlockSpec((tm, tn), lambda i, j, l: (i, l)),
                  pl.BlockSpec((None, tk, tn), lambda i, j, l: (l // r, j, l % r)), tile],
        out_specs=tile, out_shape=jax.ShapeDtypeStruct((m, k), BF), scratch_shapes=[pltpu.VMEM((tm, tk), F32)],
        compiler_params=_params(("parallel", "parallel", "arbitrary")),
    )(a, w, u)


def _mm_tn(name, a, b, g):
    m, k = a.shape
    m2, n = b.shape
    assert m == m2 and n % g == 0
    nc = n // g
    tm, tk, tn = min(m, 1024), min(k, 1024), _col_tile(nc)
    r = nc // tn
    return _mm(name, a, b, (k // tk, g * r, m // tm),
               [pl.BlockSpec((tm, tk), lambda i, j, l: (l, i)),
                pl.BlockSpec((tm, tn), lambda i, j, l: (l, j))],
               pl.BlockSpec((None, tk, tn), lambda i, j, l: (j // r, i, j % r)),
               jax.ShapeDtypeStruct((g, k, nc), F32), (((0,), (0,)), ((), ())), (tk, tn))


def _bd_nn(name, x, w):
    s, c = x.shape
    nh, hd, _ = w.shape
    tm = min(s, 1024)

    def body(x_ref, w_ref, o_ref):
        o_ref[...] = jnp.dot(x_ref[...].astype(BF), w_ref[...].astype(BF), preferred_element_type=F32)

    return pl.pallas_call(
        body, grid=(s // tm, nh), name=name,
        in_specs=[pl.BlockSpec((tm, hd), lambda i, h: (i, h)), pl.BlockSpec((None, hd, hd), lambda i, h: (h, 0, 0))],
        out_specs=pl.BlockSpec((tm, hd), lambda i, h: (i, h)),
        out_shape=jax.ShapeDtypeStruct((s, c), F32),
        compiler_params=_params(("parallel", "parallel")),
    )(x, w)


def _bd_nt2(name, add, dy1, w1, dy2, w2):
    s, c = add.shape
    nh, hd, _ = w1.shape
    tm = min(s, 1024)
    nt = (((1,), (1,)), ((), ()))

    def body(a_ref, d1_ref, w1_ref, d2_ref, w2_ref, o_ref):
        o_ref[...] = (a_ref[...]
                      + lax.dot_general(d1_ref[...].astype(BF), w1_ref[...].astype(BF), nt, preferred_element_type=F32)
                      + lax.dot_general(d2_ref[...].astype(BF), w2_ref[...].astype(BF), nt, preferred_element_type=F32))

    row = pl.BlockSpec((tm, hd), lambda i, h: (i, h))
    wsp = pl.BlockSpec((None, hd, hd), lambda i, h: (h, 0, 0))
    return pl.pallas_call(
        body, grid=(s // tm, nh), name=name, in_specs=[row, row, wsp, row, wsp], out_specs=row,
        out_shape=jax.ShapeDtypeStruct((s, c), F32), compiler_params=_params(("parallel", "parallel")),
    )(add, dy1, w1, dy2, w2)


def _bd_tn(name, x, dy, nh):
    s, c = x.shape
    hd = c // nh
    tm = min(s, 1024)
    tn = (((0,), (0,)), ((), ()))

    def body(x_ref, d_ref, o_ref):
        @pl.when(pl.program_id(1) == 0)
        def _():
            o_ref[...] = jnp.zeros_like(o_ref)
        o_ref[...] += lax.dot_general(x_ref[...].astype(BF), d_ref[...].astype(BF), tn, preferred_element_type=F32)

    row = pl.BlockSpec((tm, hd), lambda h, i: (i, h))
    return pl.pallas_call(
        body, grid=(nh, s // tm), name=name, in_specs=[row, row],
        out_specs=pl.BlockSpec((None, hd, hd), lambda h, i: (h, 0, 0)),
        out_shape=jax.ShapeDtypeStruct((nh, hd, hd), F32), compiler_params=_params(("parallel", "arbitrary")),
    )(x, dy)


CONV_HALO = 32
CONV_SUB = 16


def _conv3(name, x3, w3, b2, causal):
    s, sl, hd = x3.shape
    kw = w3.shape[0]
    tc = min(s, 256)
    nchunks = s // tc
    per = tc // CONV_HALO
    nhalo = s // CONV_HALO
    assert kw - 1 <= CONV_HALO and tc % CONV_HALO == 0 and tc % CONV_SUB == 0
    has_bias = b2 is not None

    def body(*refs):
        if has_bias:
            cur_ref, halo_ref, w_ref, b_ref, y_ref, win_ref = refs
        else:
            cur_ref, halo_ref, w_ref, y_ref, win_ref = refs
        i = pl.program_id(0)
        if causal:
            win_ref[0:CONV_HALO] = jnp.where(i > 0, halo_ref[...], 0.0)
            win_ref[CONV_HALO:CONV_HALO + tc] = cur_ref[...]
        else:
            win_ref[0:tc] = cur_ref[...]
            win_ref[tc:tc + CONV_HALO] = jnp.where(i < nchunks - 1, halo_ref[...], 0.0)

        def sub_step(j, carry):
            t0 = pl.multiple_of(j * CONV_SUB, CONV_SUB)
            if has_bias:
                acc = jnp.broadcast_to(b_ref[...], (CONV_SUB, sl, hd))
            else:
                acc = jnp.zeros((CONV_SUB, sl, hd), F32)
            for k in range(kw):
                off = CONV_HALO - (kw - 1) + k if causal else kw - 1 - k
                acc = acc + w_ref[k] * win_ref[pl.ds(t0 + off, CONV_SUB)]
            y_ref[pl.ds(t0, CONV_SUB)] = acc
            return carry

        lax.fori_loop(0, tc // CONV_SUB, sub_step, 0)

    if causal:
        halo_map = lambda i: (jnp.maximum(i * per - 1, 0), 0, 0)
    else:
        halo_map = lambda i: (jnp.minimum((i + 1) * per, nhalo - 1), 0, 0)
    chunk = pl.BlockSpec((tc, sl, hd), lambda i: (i, 0, 0))
    in_specs = [chunk, pl.BlockSpec((CONV_HALO, sl, hd), halo_map), pl.BlockSpec((kw, sl, hd), lambda i: (0, 0, 0))]
    args = [x3, x3, w3]
    if has_bias:
        in_specs.append(pl.BlockSpec((sl, hd), lambda i: (0, 0)))
        args.append(b2)
    return pl.pallas_call(
        body, grid=(nchunks,), name=name, in_specs=in_specs, out_specs=chunk,
        out_shape=jax.ShapeDtypeStruct((s, sl, hd), F32),
        scratch_shapes=[pltpu.VMEM((tc + CONV_HALO, sl, hd), F32)],
        compiler_params=_params(("parallel",)),
    )(*args)


def _conv3_bwd_w(name, dy3, x3, kw):
    s, sl, hd = x3.shape
    tc = min(s, 256)
    per = tc // CONV_HALO

    def body(dy_ref, cur_ref, halo_ref, dw_ref, db_ref, win_ref):
        i = pl.program_id(0)

        @pl.when(i == 0)
        def _():
            dw_ref[...] = jnp.zeros_like(dw_ref)
            db_ref[...] = jnp.zeros_like(db_ref)
        win_ref[0:CONV_HALO] = jnp.where(i > 0, halo_ref[...], 0.0)
        win_ref[CONV_HALO:CONV_HALO + tc] = cur_ref[...]

        def sub_step(j, carry):
            t0 = pl.multiple_of(j * SUBLANES, SUBLANES)
            dy = dy_ref[pl.ds(t0, SUBLANES)]
            new = [carry[k] + jnp.sum(dy * win_ref[pl.ds(t0 + CONV_HALO - (kw - 1) + k, SUBLANES)], axis=0)
                   for k in range(kw)]
            new.append(carry[kw] + jnp.sum(dy, axis=0))
            return tuple(new)

        zero = jnp.zeros((sl, hd), F32)
        res = lax.fori_loop(0, tc // SUBLANES, sub_step, tuple(zero for _ in range(kw + 1)))
        for k in range(kw):
            dw_ref[k] += res[k]
        db_ref[...] += res[kw]

    chunk = pl.BlockSpec((tc, sl, hd), lambda i: (i, 0, 0))
    return pl.pallas_call(
        body, grid=(s // tc,), name=name,
        in_specs=[chunk, chunk, pl.BlockSpec((CONV_HALO, sl, hd), lambda i: (jnp.maximum(i * per - 1, 0), 0, 0))],
        out_specs=[pl.BlockSpec((kw, sl, hd), lambda i: (0, 0, 0)), pl.BlockSpec((sl, hd), lambda i: (0, 0))],
        out_shape=[jax.ShapeDtypeStruct((kw, sl, hd), F32), jax.ShapeDtypeStruct((sl, hd), F32)],
        scratch_shapes=[pltpu.VMEM((tc + CONV_HALO, sl, hd), F32)],
        compiler_params=_params(("arbitrary",)),
    )(dy3, x3, x3)


def _scan_fwd(name, a3, u3):
    s, sl, hd = a3.shape
    tc = min(s, 512)

    def body(a_ref, u_ref, h_ref, carry_ref):
        @pl.when(pl.program_id(0) == 0)
        def _():
            carry_ref[...] = jnp.zeros_like(carry_ref)

        def step(t, h):
            h = a_ref[t] * h + u_ref[t]
            h_ref[t] = h
            return h

        carry_ref[...] = lax.fori_loop(0, tc, step, carry_ref[...], unroll=8)

    chunk = pl.BlockSpec((tc, sl, hd), lambda i: (i, 0, 0))
    return pl.pallas_call(
        body, grid=(s // tc,), name=name, in_specs=[chunk, chunk], out_specs=chunk,
        out_shape=jax.ShapeDtypeStruct((s, sl, hd), F32), scratch_shapes=[pltpu.VMEM((sl, hd), F32)],
        compiler_params=_params(("arbitrary",)),
    )(a3, u3)


def _scan_bwd(name, a3, gh3, hprev3):
    s, sl, hd = a3.shape
    tc = min(s, 512)
    n = s // tc

    def body(a_ref, g_ref, hp_ref, da_ref, du_ref, carry_ref):
        @pl.when(pl.program_id(0) == 0)
        def _():
            carry_ref[...] = jnp.zeros_like(carry_ref)

        def step(j, c):
            t = tc - 1 - j
            lam = g_ref[t] + c
            du_ref[t] = lam
            da_ref[t] = lam * hp_ref[t]
            return a_ref[t] * lam

        carry_ref[...] = lax.fori_loop(0, tc, step, carry_ref[...], unroll=8)

    chunk = pl.BlockSpec((tc, sl, hd), lambda i: (n - 1 - i, 0, 0))
    shp = jax.ShapeDtypeStruct((s, sl, hd), F32)
    return pl.pallas_call(
        body, grid=(n,), name=name, in_specs=[chunk, chunk, chunk], out_specs=[chunk, chunk],
        out_shape=[shp, shp], scratch_shapes=[pltpu.VMEM((sl, hd), F32)],
        compiler_params=_params(("arbitrary",)),
    )(a3, gh3, hprev3)


ATT_Q = 512
ATT_K = 128
NT_DIMS = (((1,), (1,)), ((), ()))
TN_DIMS = (((0,), (0,)), ((), ()))


def _attn_tiles(s):
    tq = min(s, ATT_Q)
    kb = min(tq, ATT_K)
    assert s % tq == 0 and tq % kb == 0
    return tq, kb


def _hilo_dot(x, u):
    hi = x.astype(BF)
    lo = (x - hi.astype(F32)).astype(BF)
    return jnp.dot(hi, u, preferred_element_type=F32) + jnp.dot(lo, u, preferred_element_type=F32)


def _attn_fwd(name, qkv, nh):
    s, d3 = qkv.shape
    d = d3 // 3
    dh = d // nh
    tq, kb = _attn_tiles(s)
    per = tq // kb
    scale = 1.0 / math.sqrt(dh)

    def body(q_ref, k_ref, v_ref, o_ref, lam_ref):
        i = pl.program_id(1)
        q = q_ref[...]
        after = (lax.broadcasted_iota(jnp.int32, (kb, kb), 0)
                 > lax.broadcasted_iota(jnp.int32, (kb, kb), 1)).astype(BF)
        qpos = i * tq + lax.broadcasted_iota(jnp.int32, (tq, kb), 0)
        kcol = lax.broadcasted_iota(jnp.int32, (tq, kb), 1)

        def block(b, carry, masked):
            acc, tail_carry = carry
            off = pl.multiple_of(b * kb, kb)
            kt = k_ref[pl.ds(off, kb), :]
            vt = v_ref[pl.ds(off, kb), :]
            z = lax.dot_general(q, kt, NT_DIMS, preferred_element_type=F32) * scale
            lraw = -(jnp.maximum(z, 0.0) + jnp.log(1.0 + jnp.exp(-jnp.abs(z))))
            if masked:
                mask = (off + kcol) < qpos
                lm = jnp.where(mask, lraw, 0.0)
            else:
                lm = lraw
            w = jnp.exp(z + lraw + _hilo_dot(lm, after) + tail_carry)
            if masked:
                w = jnp.where(mask, w, 0.0)
            acc = acc + jnp.dot(w.astype(BF), vt, preferred_element_type=F32)
            return acc, tail_carry + jnp.sum(lm, axis=1, keepdims=True)

        carry = (jnp.zeros((tq, dh), F32), jnp.zeros((tq, 1), F32))
        for jj in range(per):
            carry = block((i + 1) * per - 1 - jj, carry, True)

        def group(gi, c):
            for jj in range(per):
                c = block((i - gi) * per - 1 - jj, c, False)
            return c

        acc, total = lax.fori_loop(0, i, group, carry)
        o_ref[...] = acc.astype(o_ref.dtype)
        lam_ref[...] = total

    return pl.pallas_call(
        body, grid=(nh, s // tq), name=name,
        in_specs=[pl.BlockSpec((tq, dh), lambda h, i: (i, h)),
                  pl.BlockSpec((s, dh), lambda h, i: (0, nh + h)),
                  pl.BlockSpec((s, dh), lambda h, i: (0, 2 * nh + h))],
        out_specs=[pl.BlockSpec((tq, dh), lambda h, i: (i, h)), pl.BlockSpec((None, tq, 1), lambda h, i: (h, i, 0))],
        out_shape=[jax.ShapeDtypeStruct((s, d), BF), jax.ShapeDtypeStruct((nh, s, 1), F32)],
        compiler_params=_params(("parallel", "arbitrary")),
    )(qkv, qkv, qkv)


def _attn_bwd(name, qkv, lam, do, nh):
    s, d3 = qkv.shape
    d = d3 // 3
    dh = d // nh
    tq, kb = _attn_tiles(s)
    per = tq // kb
    scale = 1.0 / math.sqrt(dh)

    def body(q_ref, k_ref, v_ref, do_ref, lam_ref, dq_ref, dk_ref, dv_ref):
        i = pl.program_id(1)

        @pl.when(i == 0)
        def _():
            dk_ref[...] = jnp.zeros_like(dk_ref)
            dv_ref[...] = jnp.zeros_like(dv_ref)
        q = q_ref[...]
        dout = do_ref[...]
        total = lam_ref[...]
        row = lax.broadcasted_iota(jnp.int32, (kb, kb), 0)
        col = lax.broadcasted_iota(jnp.int32, (kb, kb), 1)
        upto = (row <= col).astype(BF)
        before = (row < col).astype(BF)
        qpos = i * tq + lax.broadcasted_iota(jnp.int32, (tq, kb), 0)
        kcol = lax.broadcasted_iota(jnp.int32, (tq, kb), 1)

        def block(b, carry, masked):
            dq, l_carry, g_carry = carry
            off = pl.multiple_of(b * kb, kb)
            kt = k_ref[pl.ds(off, kb), :]
            vt = v_ref[pl.ds(off, kb), :]
            z = lax.dot_general(q, kt, NT_DIMS, preferred_element_type=F32) * scale
            lraw = -(jnp.maximum(z, 0.0) + jnp.log(1.0 + jnp.exp(-jnp.abs(z))))
            if masked:
                mask = (off + kcol) < qpos
                lm = jnp.where(mask, lraw, 0.0)
            else:
                lm = lraw
            logsig = z + lraw
            w = jnp.exp(logsig + total - (_hilo_dot(lm, upto) + l_carry))
            if masked:
                w = jnp.where(mask, w, 0.0)
            g = w * lax.dot_general(dout, vt, NT_DIMS, preferred_element_type=F32)
            sg = jnp.exp(logsig)
            dz = g * (1.0 - sg) - sg * (_hilo_dot(g, before) + g_carry)
            if masked:
                dz = jnp.where(mask, dz, 0.0)
            dz = (dz * scale).astype(BF)
            dq = dq + jnp.dot(dz, kt, preferred_element_type=F32)
            dk_ref[pl.ds(off, kb), :] += lax.dot_general(dz, q, TN_DIMS, preferred_element_type=F32)
            dv_ref[pl.ds(off, kb), :] += lax.dot_general(w.astype(BF), dout, TN_DIMS, preferred_element_type=F32)
            return (dq, l_carry + jnp.sum(lm, axis=1, keepdims=True), g_carry + jnp.sum(g, axis=1, keepdims=True))

        zero_col = jnp.zeros((tq, 1), F32)
        carry = (jnp.zeros((tq, dh), F32), zero_col, zero_col)

        def group(gi, c):
            for jj in range(per):
                c = block(gi * per + jj, c, False)
            return c

        carry = lax.fori_loop(0, i, group, carry)
        for jj in range(per):
            carry = block(i * per + jj, carry, True)
        dq_ref[...] = carry[0]

    blk = pl.BlockSpec((tq, dh), lambda h, i: (i, h))
    head = pl.BlockSpec((s, dh), lambda h, i: (0, h))
    shp = jax.ShapeDtypeStruct((s, d), F32)
    return pl.pallas_call(
        body, grid=(nh, s // tq), name=name,
        in_specs=[blk, pl.BlockSpec((s, dh), lambda h, i: (0, nh + h)),
                  pl.BlockSpec((s, dh), lambda h, i: (0, 2 * nh + h)), blk,
                  pl.BlockSpec((None, tq, 1), lambda h, i: (h, i, 0))],
        out_specs=[blk, head, head], out_shape=[shp, shp, shp],
        compiler_params=_params(("parallel", "arbitrary")),
    )(qkv, qkv, qkv, do, lam)


def _adamw(name, w, g, m, v):
    shape = w.shape
    c = shape[-1]
    r = int(math.prod(shape[:-1])) if len(shape) > 1 else 1
    w2, g2, m2, v2 = (a.reshape(r, c) for a in (w, g, m, v))
    t = min(r, max(SUBLANES, _pow2_floor((1 << 19) // (4 * c))))
    if r % t:
        t = r
    assert r * c * 4 <= (1 << 22) or t < r

    def body(w_ref, g_ref, m_ref, v_ref, d_ref, nm_ref, nv_ref):
        gg = g_ref[...]
        nm = ADAM_B1 * m_ref[...] + (1.0 - ADAM_B1) * gg
        nv = ADAM_B2 * v_ref[...] + (1.0 - ADAM_B2) * (gg * gg)
        m_hat = nm / (1.0 - ADAM_B1 ** ADAM_STEP)
        v_hat = nv / (1.0 - ADAM_B2 ** ADAM_STEP)
        d_ref[...] = -ADAM_LR * (m_hat / (jnp.sqrt(v_hat) + ADAM_EPS) + ADAM_WD * w_ref[...])
        nm_ref[...] = nm
        nv_ref[...] = nv

    spec = pl.BlockSpec((t, c), lambda i: (i, 0))
    shp = jax.ShapeDtypeStruct((r, c), F32)
    d, nm, nv = pl.pallas_call(
        body, grid=(r // t,), name=name, in_specs=[spec] * 4, out_specs=[spec] * 3, out_shape=[shp] * 3,
        compiler_params=_params(("parallel",)),
    )(w2, g2, m2, v2)
    return d.reshape(shape), nm.reshape(shape), nv.reshape(shape)


def _add_row_tile(m, c):
    t = min(m, max(SUBLANES, _pow2_floor((1 << 18) // c)))
    assert m % t == 0
    return t


def _add_half(name, g, theirs, core, out_dtype):
    nb, _, m, c = g.shape
    t = _add_row_tile(m, c)

    def body(core_ref, a_ref, b_ref, o_ref):
        o_ref[...] = (a_ref[...] + b_ref[...]).astype(o_ref.dtype)

    spec = pl.BlockSpec((None, t, c), lambda k, i, core_ref: (k, i, 0))
    return pl.pallas_call(
        body, name=name, out_shape=jax.ShapeDtypeStruct((nb, m, c), out_dtype),
        grid_spec=pltpu.PrefetchScalarGridSpec(
            num_scalar_prefetch=1, grid=(nb, m // t),
            in_specs=[pl.BlockSpec((None, None, t, c), lambda k, i, core_ref: (k, core_ref[0], i, 0)), spec],
            out_specs=spec),
        compiler_params=_params(("parallel", "parallel")))(core, g, theirs)


def _sum_chips(name, own, landed, chip):
    _, m, c = own.shape
    t = _add_row_tile(m, c)

    def body(chip_ref, own_ref, land_ref, o_ref):
        me = chip_ref[0]
        mine = own_ref[...].astype(F32)
        acc = None
        for q in range(N_CHIPS):
            rel = jnp.bitwise_xor(me, q)
            slot = jnp.where(rel == 2, 0, jnp.where(rel == 1, 1, 2))
            term = jnp.where(rel == 0, mine, land_ref[slot].astype(F32))
            acc = term if acc is None else acc + term
        o_ref[...] = acc

    return pl.pallas_call(
        body, name=name, out_shape=jax.ShapeDtypeStruct((m, c), F32),
        grid_spec=pltpu.PrefetchScalarGridSpec(
            num_scalar_prefetch=1, grid=(m // t,),
            in_specs=[pl.BlockSpec((None, t, c), lambda i, chip_ref: (chip_ref[0], i, 0)),
                      pl.BlockSpec((3, t, c), lambda i, chip_ref: (0, i, 0))],
            out_specs=pl.BlockSpec((t, c), lambda i, chip_ref: (i, 0))),
        compiler_params=_params(("parallel",)))(chip, own, landed)


HBM_SPEC = pl.BlockSpec(memory_space=pltpu.HBM)


def _me():
    return lax.axis_index("x"), lax.axis_index("y"), lax.axis_index("c")


def _remote(src, dst, send_sem, recv_sem, dev):
    return pltpu.make_async_remote_copy(src_ref=src, dst_ref=dst, send_sem=send_sem, recv_sem=recv_sem,
                                        device_id=dev, device_id_type=MESH)


def _other_chips(x, y):
    return [(px, py, 2 * px + py) for px, py in ((1 - x, y), (x, 1 - y), (1 - x, 1 - y))]


def _all_gather_shards(name, shards):
    nw = len(shards)
    halves = [sh.shape[0] // 2 for sh in shards]

    def body(*refs):
        x_refs, out_refs = refs[:nw], refs[nw:2 * nw]
        send_sems, recv_sems = refs[2 * nw:]
        x, y, c = _me()
        me, sibling = (x, y, c), (x, y, 1 - c)
        chips = _other_chips(x, y)
        mychip = 2 * x + y

        def blk(i, q, pc):
            return out_refs[i].at[q, pl.ds(pc * halves[i], halves[i]), :]

        def half(i):
            return x_refs[i].at[pl.ds(c * halves[i], halves[i]), :]

        first = [_remote(half(i), blk(i, mychip, c), send_sems.at[6 * i + k], recv_sems.at[6 * i + k], (px, py, c))
                 for i in range(nw) for k, (px, py, q) in enumerate(chips)]
        for cp in first:
            cp.start()
        passed = []
        for i in range(nw):
            for k, (px, py, q) in enumerate(chips):
                _remote(half(i), blk(i, q, c), send_sems.at[6 * i + k], recv_sems.at[6 * i + k], me).wait_recv()
                fwd = _remote(blk(i, q, c), blk(i, q, c), send_sems.at[6 * i + 3 + k], recv_sems.at[6 * i + 3 + k],
                              sibling)
                fwd.start()
                passed.append(fwd)
        for i in range(nw):
            for k, (px, py, q) in enumerate(chips):
                _remote(half(i), blk(i, q, 1 - c), send_sems.at[6 * i + 3 + k], recv_sems.at[6 * i + 3 + k],
                        me).wait_recv()
        for cp in first + passed:
            cp.wait_send()

    return pl.pallas_call(
        body, name=name,
        out_shape=[jax.ShapeDtypeStruct((N_CHIPS,) + tuple(sh.shape), sh.dtype) for sh in shards],
        in_specs=[HBM_SPEC] * nw, out_specs=[HBM_SPEC] * nw,
        scratch_shapes=[pltpu.SemaphoreType.DMA((6 * nw,)), pltpu.SemaphoreType.DMA((6 * nw,))],
    )(*shards)


def _pair_swap(name, srcs):
    nw = len(srcs)

    def body(*refs):
        src_refs, out_refs = refs[:nw], refs[nw:2 * nw]
        send_sems, recv_sems = refs[2 * nw:]
        x, y, c = _me()
        cps = [_remote(src_refs[i].at[k, 1 - c], out_refs[i].at[k], send_sems.at[N_CHIPS * i + k],
                       recv_sems.at[N_CHIPS * i + k], (x, y, 1 - c))
               for i in range(nw) for k in range(N_CHIPS)]
        for cp in cps:
            cp.start()
        for cp in cps:
            cp.wait()

    return pl.pallas_call(
        body, name=name,
        out_shape=[jax.ShapeDtypeStruct((N_CHIPS,) + tuple(s.shape[2:]), s.dtype) for s in srcs],
        in_specs=[HBM_SPEC] * nw, out_specs=[HBM_SPEC] * nw,
        scratch_shapes=[pltpu.SemaphoreType.DMA((N_CHIPS * nw,)), pltpu.SemaphoreType.DMA((N_CHIPS * nw,))],
    )(*srcs)


def _chip_all_to_all(name, ps):
    nw = len(ps)

    def body(*refs):
        p_refs, out_refs = refs[:nw], refs[nw:2 * nw]
        send_sems, recv_sems = refs[2 * nw:]
        x, y, c = _me()
        chips = _other_chips(x, y)
        cps = [_remote(p_refs[i].at[q], out_refs[i].at[k], send_sems.at[3 * i + k], recv_sems.at[3 * i + k],
                       (px, py, c))
               for i in range(nw) for k, (px, py, q) in enumerate(chips)]
        for cp in cps:
            cp.start()
        for cp in cps:
            cp.wait()

    return pl.pallas_call(
        body, name=name, out_shape=[jax.ShapeDtypeStruct((3,) + tuple(p.shape[1:]), p.dtype) for p in ps],
        in_specs=[HBM_SPEC] * nw, out_specs=[HBM_SPEC] * nw,
        scratch_shapes=[pltpu.SemaphoreType.DMA((3 * nw,)), pltpu.SemaphoreType.DMA((3 * nw,))],
    )(*ps)


def _pair_send(name, hs):
    nw = len(hs)

    def body(*refs):
        h_refs, out_refs = refs[:nw], refs[nw:2 * nw]
        send_sems, recv_sems = refs[2 * nw:]
        x, y, c = _me()
        cps = [_remote(h_refs[i], out_refs[i], send_sems.at[i], recv_sems.at[i], (x, y, 1 - c)) for i in range(nw)]
        for cp in cps:
            cp.start()
        for cp in cps:
            cp.wait()

    return pl.pallas_call(
        body, name=name, out_shape=[jax.ShapeDtypeStruct(tuple(h.shape), h.dtype) for h in hs],
        in_specs=[HBM_SPEC] * nw, out_specs=[HBM_SPEC] * nw,
        scratch_shapes=[pltpu.SemaphoreType.DMA((nw,)), pltpu.SemaphoreType.DMA((nw,))],
    )(*hs)


def _small_allreduce(name, buf):
    r, n = buf.shape

    def body(x_ref, o_ref, land_ref, send_sems, recv_sems):
        x, y, c = _me()
        me = 4 * x + 2 * y + c
        land_ref[me] = x_ref[...]
        peers = []
        for k in range(1, N_DEV):
            px = 1 - x if k & 4 else x
            py = 1 - y if k & 2 else y
            pc = 1 - c if k & 1 else c
            peers.append((px, py, pc))
        cps = [_remote(x_ref, land_ref.at[me], send_sems.at[k], recv_sems.at[k], peer)
               for k, peer in enumerate(peers)]
        for cp in cps:
            cp.start()
        for k, (px, py, pc) in enumerate(peers):
            _remote(x_ref, land_ref.at[4 * px + 2 * py + pc], send_sems.at[k], recv_sems.at[k],
                    (px, py, pc)).wait_recv()
        for cp in cps:
            cp.wait_send()
        acc = land_ref[0]
        for q in range(1, N_DEV):
            acc = acc + land_ref[q]
        o_ref[...] = acc

    vmem = pl.BlockSpec(memory_space=pltpu.VMEM)
    return pl.pallas_call(
        body, name=name, out_shape=jax.ShapeDtypeStruct((r, n), F32), in_specs=[vmem], out_specs=vmem,
        scratch_shapes=[pltpu.VMEM((N_DEV, r, n), F32), pltpu.SemaphoreType.DMA((N_DEV - 1,)),
                        pltpu.SemaphoreType.DMA((N_DEV - 1,))],
        compiler_params=pltpu.CompilerParams(vmem_limit_bytes=VMEM_LIMIT_BYTES),
    )(buf)


def _small_layout(shapes):
    offs, off = {}, 0
    for name in SMALL:
        n = int(math.prod(shapes[name]))
        offs[name] = (off, n)
        off += n
    rows = -(-off // (SUBLANES * LANES)) * SUBLANES
    return offs, rows


def _pack_small(vals, offs, rows):
    parts = []
    for name in SMALL:
        off, n = offs[name]
        parts.append(vals[name].reshape(n).astype(F32) if name in vals else jnp.zeros((n,), F32))
    used = sum(p.shape[0] for p in parts)
    parts.append(jnp.zeros((rows * LANES - used,), F32))
    return jnp.concatenate(parts).reshape(rows, LANES)


def _unpack_small(buf, offs, shapes):
    flat = buf.reshape(-1)
    return {name: flat[offs[name][0]:offs[name][0] + offs[name][1]].reshape(shapes[name]) for name in SMALL}


def kernel(x, p, norm_mix_g, norm_mlp_g, norm_ple_g, norm_f_g, w_in_rec, conv_a_w, conv_a_b, ln_a_g, ln_a_b, conv_b_w, conv_b_b, w_rg_a, b_rg_a, w_rg_x, b_rg_x, rg_lambda, w_out_rec, w_qkv, w_o_attn, w_mlp_up, w_mlp_down, w_ple_proj, w_ple_gate, loss_target, m_norm_mix_g, m_norm_mlp_g, m_norm_ple_g, m_norm_f_g, m_w_in_rec, m_conv_a_w, m_conv_a_b, m_ln_a_g, m_ln_a_b, m_conv_b_w, m_conv_b_b, m_w_rg_a, m_b_rg_a, m_w_rg_x, m_b_rg_x, m_rg_lambda, m_w_out_rec, m_w_qkv, m_w_o_attn, m_w_mlp_up, m_w_mlp_down, m_w_ple_proj, m_w_ple_gate, v_norm_mix_g, v_norm_mlp_g, v_norm_ple_g, v_norm_f_g, v_w_in_rec, v_conv_a_w, v_conv_a_b, v_ln_a_g, v_ln_a_b, v_conv_b_w, v_conv_b_b, v_w_rg_a, v_b_rg_a, v_w_rg_x, v_b_rg_x, v_rg_lambda, v_w_out_rec, v_w_qkv, v_w_o_attn, v_w_mlp_up, v_w_mlp_down, v_w_ple_proj, v_w_ple_gate):
    wts = dict(norm_mix_g=norm_mix_g, norm_mlp_g=norm_mlp_g, norm_ple_g=norm_ple_g, norm_f_g=norm_f_g,
               w_in_rec=w_in_rec, conv_a_w=conv_a_w, conv_a_b=conv_a_b, ln_a_g=ln_a_g, ln_a_b=ln_a_b,
               conv_b_w=conv_b_w, conv_b_b=conv_b_b, w_rg_a=w_rg_a, b_rg_a=b_rg_a, w_rg_x=w_rg_x, b_rg_x=b_rg_x,
               rg_lambda=rg_lambda, w_out_rec=w_out_rec, w_qkv=w_qkv, w_o_attn=w_o_attn, w_mlp_up=w_mlp_up,
               w_mlp_down=w_mlp_down, w_ple_proj=w_ple_proj, w_ple_gate=w_ple_gate)
    mom = dict(norm_mix_g=m_norm_mix_g, norm_mlp_g=m_norm_mlp_g, norm_ple_g=m_norm_ple_g, norm_f_g=m_norm_f_g,
               w_in_rec=m_w_in_rec, conv_a_w=m_conv_a_w, conv_a_b=m_conv_a_b, ln_a_g=m_ln_a_g, ln_a_b=m_ln_a_b,
               conv_b_w=m_conv_b_w, conv_b_b=m_conv_b_b, w_rg_a=m_w_rg_a, b_rg_a=m_b_rg_a, w_rg_x=m_w_rg_x,
               b_rg_x=m_b_rg_x, rg_lambda=m_rg_lambda, w_out_rec=m_w_out_rec, w_qkv=m_w_qkv, w_o_attn=m_w_o_attn,
               w_mlp_up=m_w_mlp_up, w_mlp_down=m_w_mlp_down, w_ple_proj=m_w_ple_proj, w_ple_gate=m_w_ple_gate)
    var = dict(norm_mix_g=v_norm_mix_g, norm_mlp_g=v_norm_mlp_g, norm_ple_g=v_norm_ple_g, norm_f_g=v_norm_f_g,
               w_in_rec=v_w_in_rec, conv_a_w=v_conv_a_w, conv_a_b=v_conv_a_b, ln_a_g=v_ln_a_g, ln_a_b=v_ln_a_b,
               conv_b_w=v_conv_b_w, conv_b_b=v_conv_b_b, w_rg_a=v_w_rg_a, b_rg_a=v_b_rg_a, w_rg_x=v_w_rg_x,
               b_rg_x=v_b_rg_x, rg_lambda=v_rg_lambda, w_out_rec=v_w_out_rec, w_qkv=v_w_qkv, w_o_attn=v_w_o_attn,
               w_mlp_up=v_w_mlp_up, w_mlp_down=v_w_mlp_down, w_ple_proj=v_w_ple_proj, w_ple_gate=v_w_ple_gate)

    xi, yi, ci = lax.axis_index("x"), lax.axis_index("y"), lax.axis_index("c")
    chip = 2 * xi + yi
    seq, dm = x.shape[1], x.shape[2]
    depth = norm_mix_g.shape[0]
    dh2 = dm // 2
    hd = dh2 // RG_HEADS
    kw_a, kw_b = conv_a_w.shape[1], conv_b_w.shape[1]
    cshard = conv_a_w.shape[2]

    small_shapes = {n: tuple(wts[n].shape) for n in SMALL}
    small_shapes["conv_a_w"] = (conv_a_w.shape[0], kw_a, dh2)
    small_shapes["conv_b_w"] = (conv_b_w.shape[0], kw_b, dh2)
    offs, small_rows = _small_layout(small_shapes)
    south = (ci == 0).astype(F32)
    placed = {}
    for n in ("conv_a_w", "conv_b_w"):
        placed[n] = lax.dynamic_update_slice(jnp.zeros(small_shapes[n], F32), wts[n] * south,
                                             (0, 0, chip * cshard))
    conv_full = _unpack_small(_small_allreduce("small_ar_conv", _pack_small(placed, offs, small_rows)),
                              offs, small_shapes)
    conv_a_full, conv_b_full = conv_full["conv_a_w"], conv_full["conv_b_w"]

    def layer_mats(i):
        j = i // 2
        mix = ["w_in_rec", "w_out_rec"] if i % 2 == 0 else ["w_qkv", "w_o_attn"]
        names = mix + ["w_mlp_up", "w_mlp_down", "w_ple_proj", "w_ple_gate"]
        return [(n, j if n in mix else i) for n in names]

    core = ci.astype(jnp.int32).reshape(1)
    chip_id = chip.astype(jnp.int32).reshape(1)
    row_half = jnp.arange(2, dtype=ci.dtype).reshape(2, 1, 1)

    def gather_layer(i):
        mats = layer_mats(i)
        shards = [wts[n][idx].astype(BF) for n, idx in mats]
        fulls = _all_gather_shards(f"ag_l{i}", shards)
        out = {}
        for (n, idx), shard, full in zip(mats, shards, fulls):
            kk, nn = wts[n].shape[1:]
            full = lax.dynamic_update_slice(full, shard[None], (chip, 0, 0))
            out[n] = full if n in COL_SHARDED else full.reshape(1, N_CHIPS * kk, nn)
        return out

    def reduce_layer(i, grads):
        mats = layer_mats(i)
        parts = []
        for n, _ in mats:
            kk, nn = wts[n].shape[1:]
            parts.append(grads[n].reshape(N_CHIPS, 2, kk // 2, nn))
        theirs = _pair_swap(f"rs_pair_l{i}", parts)
        chip_sums = [_add_half(f"rs_add_{n}_l{i}", g, t, core, BF) for (n, _), g, t in zip(mats, parts, theirs)]
        landed = _chip_all_to_all(f"rs_a2a_l{i}", chip_sums)
        mine = [_sum_chips(f"rs_sum_{n}_l{i}", own, a, chip_id) for (n, _), own, a in zip(mats, chip_sums, landed)]
        others = _pair_send(f"rs_gather_l{i}", mine)
        out = {}
        for (n, _), a, b in zip(mats, mine, others):
            both = jnp.where(row_half == ci, a[None], b[None])
            out[n] = both.reshape(wts[n].shape[1:])
        return out

    row2 = lambda a: a.reshape(1, -1)
    to3 = lambda a: a.reshape(seq, RG_HEADS, hd)
    to2 = lambda a: a.reshape(seq, dh2)

    h = x[0]
    saved = []
    for i in range(depth):
        j = i // 2
        w = gather_layer(i)
        sv = dict(w=w, h=h)
        (hn,) = _rows_fwd(f"norm_mix_l{i}", f_norm, [h], [row2(norm_mix_g[i])], [((dm,), BF)])
        sv["hn"] = hn
        if i % 2 == 0:
            u = _mm_nn(f"in_rec_l{i}", hn, w["w_in_rec"], F32)
            (v,) = _rows_fwd(f"glu_l{i}", f_glu, [Cols(u, 0, dh2), Cols(u, 1, dh2)], [], [((dh2,), F32)])
            v3 = to3(v)
            yc = to2(_conv3(f"conv_a_l{i}", v3, conv_a_full[j].reshape(kw_a, RG_HEADS, hd),
                            conv_a_b[j].reshape(RG_HEADS, hd), True))
            (ya,) = _rows_fwd(f"ln_silu_l{i}", f_ln_silu, [yc], [row2(ln_a_g[j]), row2(ln_a_b[j])], [((dh2,), BF)])
            xr3 = to3(u[:, 2 * dh2:3 * dh2])
            xc = to2(_conv3(f"conv_b_l{i}", xr3, conv_b_full[j].reshape(kw_b, RG_HEADS, hd),
                            conv_b_b[j].reshape(RG_HEADS, hd), True))
            ra = _bd_nn(f"rg_a_l{i}", xc, w_rg_a[j])
            ix = _bd_nn(f"rg_x_l{i}", xc, w_rg_x[j])
            gate_params = [row2(b_rg_a[j]), row2(b_rg_x[j]), row2(rg_lambda[j])]
            a, uu = _rows_fwd(f"gates_l{i}", f_gates, [ra, ix, xc], gate_params, [((dh2,), F32), ((dh2,), F32)])
            hs3 = _scan_fwd(f"scan_l{i}", to3(a), to3(uu))
            hs = to2(hs3)
            (yb,) = _rows_fwd(f"gelu_gate_l{i}", f_gelu_gate, [hs, Cols(u, 3, dh2)], [], [((dh2,), BF)])
            cat = jnp.concatenate([ya, yb], axis=1)
            mix = _mm_nn(f"out_rec_l{i}", cat, w["w_out_rec"], F32)
            sv.update(u=u, v3=v3, yc=yc, xr3=xr3, xc=xc, ra=ra, ix=ix, a=a, hs3=hs3, hs=hs, cat=cat,
                      gate_params=gate_params)
        else:
            qkv = _mm_nn(f"qkv_l{i}", hn, w["w_qkv"], BF)
            o, lam = _attn_fwd(f"attn_l{i}", qkv, SB_HEADS)
            mix = _mm_nn(f"o_attn_l{i}", o, w["w_o_attn"], F32)
            sv.update(qkv=qkv, o=o, lam=lam)
        h1, hn2 = _rows_fwd(f"norm_mlp_l{i}", f_add_norm, [h, mix], [row2(norm_mlp_g[i])], [((dm,), F32), ((dm,), BF)])
        up, act = _mm_nn_relu2(f"mlp_up_l{i}", hn2, w["w_mlp_up"])
        mlp = _mm_nn(f"mlp_down_l{i}", act, w["w_mlp_down"], F32)
        h2, hn3 = _rows_fwd(f"norm_ple_l{i}", f_add_norm, [h1, mlp], [row2(norm_ple_g[i])], [((dm,), F32), ((dm,), BF)])
        gpre = _mm_nn(f"ple_gate_l{i}", hn3, w["w_ple_gate"], F32)
        pb = p[i, 0].astype(BF)
        pp = _mm_nn(f"ple_proj_l{i}", pb, w["w_ple_proj"], F32)
        (h3,) = _rows_fwd(f"ple_l{i}", f_ple, [h2, pp, gpre], [], [((dm,), F32)])
        sv.update(mix=mix, h1=h1, hn2=hn2, up=up, act=act, mlp=mlp, h2=h2, hn3=hn3, gpre=gpre, pb=pb, pp=pp)
        saved.append(sv)
        h = h3

    loss_vec, dh, g_norm_f = _loss_and_grad("loss_head", h, loss_target[0], row2(norm_f_g))
    loss = lax.psum(loss_vec[0, 0], ("x", "y", "c"))

    big_grads = {n: [None] * wts[n].shape[0] for n in BIG}
    small_grads = {n: [None] * wts[n].shape[0] for n in SMALL if n != "norm_f_g"}
    for i in reversed(range(depth)):
        j = i // 2
        sv = saved[i]
        w = sv["w"]
        lg = {}
        (d_h2, d_pp, d_gpre), _ = _rows_bwd(f"ple_bwd_l{i}", f_ple, [sv["h2"], sv["pp"], sv["gpre"]], [], [dh],
                                            [F32, BF, BF])
        lg["w_ple_proj"] = _mm_tn(f"ple_proj_dw_l{i}", sv["pb"], d_pp, N_CHIPS)
        lg["w_ple_gate"] = _mm_tn(f"ple_gate_dw_l{i}", sv["hn3"], d_gpre, 1)
        d_hn3 = _mm_nt(f"ple_gate_dx_l{i}", d_gpre, w["w_ple_gate"], F32)
        (d_h1, d_mlp), (g_ple,) = _rows_bwd(f"norm_ple_bwd_l{i}", f_add_norm, [sv["h1"], sv["mlp"]],
                                            [row2(norm_ple_g[i])], [d_h2, d_hn3], [F32, BF])
        lg["w_mlp_down"] = _mm_tn(f"mlp_down_dw_l{i}", sv["act"], d_mlp, 1)
        d_up = _mm_nt_relu2_bwd(f"mlp_down_dx_l{i}", d_mlp, w["w_mlp_down"], sv["up"])
        lg["w_mlp_up"] = _mm_tn(f"mlp_up_dw_l{i}", sv["hn2"], d_up, N_CHIPS)
        d_hn2 = _mm_nt(f"mlp_up_dx_l{i}", d_up, w["w_mlp_up"], F32)
        (d_h0, d_mix), (g_mlp,) = _rows_bwd(f"norm_mlp_bwd_l{i}", f_add_norm, [sv["h"], sv["mix"]],
                                            [row2(norm_mlp_g[i])], [d_h1, d_hn2], [F32, BF])
        if i % 2 == 0:
            u = sv["u"]
            lg["w_out_rec"] = _mm_tn(f"out_rec_dw_l{i}", sv["cat"], d_mix, 1)
            d_cat = _mm_nt(f"out_rec_dx_l{i}", d_mix, w["w_out_rec"], F32)
            (d_yc,), (g_ln_g, g_ln_b) = _rows_bwd(f"ln_silu_bwd_l{i}", f_ln_silu, [sv["yc"]],
                                                  [row2(ln_a_g[j]), row2(ln_a_b[j])], [Cols(d_cat, 0, dh2)], [F32])
            d_yc3 = to3(d_yc)
            wa3 = conv_a_full[j].reshape(kw_a, RG_HEADS, hd)
            d_v = to2(_conv3(f"conv_a_dx_l{i}", d_yc3, wa3, None, False))
            g_ca_w, g_ca_b = _conv3_bwd_w(f"conv_a_dw_l{i}", d_yc3, sv["v3"], kw_a)
            (d_aval, d_agate), _ = _rows_bwd(f"glu_bwd_l{i}", f_glu, [Cols(u, 0, dh2), Cols(u, 1, dh2)], [], [d_v],
                                             [BF, BF])
            (d_hs, d_gr), _ = _rows_bwd(f"gelu_gate_bwd_l{i}", f_gelu_gate, [sv["hs"], Cols(u, 3, dh2)], [],
                                        [Cols(d_cat, 1, dh2)], [F32, BF])
            hprev3 = jnp.concatenate([jnp.zeros((1, RG_HEADS, hd), F32), sv["hs3"][:-1]], axis=0)
            da3, du3 = _scan_bwd(f"scan_bwd_l{i}", to3(sv["a"]), to3(d_hs), hprev3)
            (d_ra, d_ix, d_xc0), (g_ba, g_bx, g_lam) = _rows_bwd(
                f"gates_bwd_l{i}", f_gates, [sv["ra"], sv["ix"], sv["xc"]], sv["gate_params"],
                [to2(da3), to2(du3)], [BF, BF, F32])
            g_wa = _bd_tn(f"rg_a_dw_l{i}", sv["xc"], d_ra, RG_HEADS)
            g_wx = _bd_tn(f"rg_x_dw_l{i}", sv["xc"], d_ix, RG_HEADS)
            d_xc = _bd_nt2(f"rg_dx_l{i}", d_xc0, d_ra, w_rg_a[j], d_ix, w_rg_x[j])
            d_xc3 = to3(d_xc)
            wb3 = conv_b_full[j].reshape(kw_b, RG_HEADS, hd)
            d_xr = to2(_conv3(f"conv_b_dx_l{i}", d_xc3, wb3, None, False))
            g_cb_w, g_cb_b = _conv3_bwd_w(f"conv_b_dw_l{i}", d_xc3, sv["xr3"], kw_b)
            d_u = jnp.concatenate([d_aval, d_agate, d_xr.astype(BF), d_gr], axis=1)
            lg["w_in_rec"] = _mm_tn(f"in_rec_dw_l{i}", sv["hn"], d_u, N_CHIPS)
            d_hn = _mm_nt(f"in_rec_dx_l{i}", d_u, w["w_in_rec"], F32)
            small_grads["conv_a_w"][j] = g_ca_w.reshape(kw_a, dh2)
            small_grads["conv_a_b"][j] = g_ca_b.reshape(dh2)
            small_grads["ln_a_g"][j] = g_ln_g.reshape(dh2)
            small_grads["ln_a_b"][j] = g_ln_b.reshape(dh2)
            small_grads["conv_b_w"][j] = g_cb_w.reshape(kw_b, dh2)
            small_grads["conv_b_b"][j] = g_cb_b.reshape(dh2)
            small_grads["w_rg_a"][j] = g_wa
            small_grads["b_rg_a"][j] = g_ba.reshape(dh2)
            small_grads["w_rg_x"][j] = g_wx
            small_grads["b_rg_x"][j] = g_bx.reshape(dh2)
            small_grads["rg_lambda"][j] = g_lam.reshape(dh2)
        else:
            lg["w_o_attn"] = _mm_tn(f"o_attn_dw_l{i}", sv["o"], d_mix, 1)
            d_o = _mm_nt(f"o_attn_dx_l{i}", d_mix, w["w_o_attn"], BF)
            dq, dk, dv = _attn_bwd(f"attn_bwd_l{i}", sv["qkv"], sv["lam"], d_o, SB_HEADS)
            d_qkv = jnp.concatenate([dq, dk, dv], axis=1).astype(BF)
            lg["w_qkv"] = _mm_tn(f"qkv_dw_l{i}", sv["hn"], d_qkv, N_CHIPS)
            d_hn = _mm_nt(f"qkv_dx_l{i}", d_qkv, w["w_qkv"], F32)
        (dh,), (g_mix,) = _rows_bwd(f"norm_mix_bwd_l{i}", f_norm, [sv["h"]], [row2(norm_mix_g[i])], [d_hn], [F32],
                                    addend=d_h0)
        small_grads["norm_mix_g"][i] = g_mix.reshape(dm)
        small_grads["norm_mlp_g"][i] = g_mlp.reshape(dm)
        small_grads["norm_ple_g"][i] = g_ple.reshape(dm)
        reduced = reduce_layer(i, lg)
        for n, idx in layer_mats(i):
            big_grads[n][idx] = reduced[n]
    grad_x = dh[None]

    small_vals = {n: jnp.stack(small_grads[n]) for n in small_grads}
    small_vals["norm_f_g"] = g_norm_f.reshape(dm)
    reduced_small = _unpack_small(_small_allreduce("small_ar_grads", _pack_small(small_vals, offs, small_rows)),
                                  offs, small_shapes)
    grads = {}
    for n in SMALL:
        g = reduced_small[n]
        if n in ("conv_a_w", "conv_b_w"):
            g = lax.dynamic_slice_in_dim(g, chip * cshard, cshard, axis=2)
        grads[n] = g
    for n in BIG:
        grads[n] = jnp.stack(big_grads[n])

    delta, new_m, new_v = {}, {}, {}
    for n in WEIGHTS:
        delta[n], new_m[n], new_v[n] = _adamw(f"adamw_{n}", wts[n], grads[n], mom[n], var[n])
    return (loss, grad_x, *[grads[n] for n in WEIGHTS], *[delta[n] for n in WEIGHTS],
            *[new_m[n] for n in WEIGHTS], *[new_v[n] for n in WEIGHTS])
```
